```python
import jax, jax.numpy as jnp
from jax import lax
import numpy as np

D_MODEL = 1024
BATCH = 4
SEQ = 4096
DEPTH = 2
DEC_BATCH = 32
DEC_SEQ = 4
PAST_LEN = 16384
PAGE_SIZE = 128

HEAD_DIM = 64
D_A = D_MODEL // 2
D_B = D_MODEL // 2
N_HEADS_B = D_B // HEAD_DIM
CONV_A = 3
D_C = D_MODEL
CONV_C = 31
D_FF = ((8 * D_MODEL // 3 + 127) // 128) * 128
CONV_FFN = 3
Q_BLOCK = 128
N_EVEN = (DEPTH + 1) // 2
N_ODD = DEPTH // 2
EPS = 1e-6
D_IN0 = 3 * D_A + 3 * D_B + N_HEADS_B

kernel_name = 'hybrid_shortconv_fox_conformer_convffn_step'


def rmsnorm(x, g):
    xf = x.astype(jnp.float32)
    y = xf * lax.rsqrt(jnp.mean(xf * xf, axis=-1, keepdims=True) + EPS)
    return (y * g.astype(jnp.float32)).astype(x.dtype)


def layernorm(x, g, b):
    xf = x.astype(jnp.float32)
    mu = jnp.mean(xf, axis=-1, keepdims=True)
    xc = xf - mu
    y = xc * lax.rsqrt(jnp.mean(xc * xc, axis=-1, keepdims=True) + EPS)
    return (y * g.astype(jnp.float32) + b.astype(jnp.float32)).astype(x.dtype)


def causal_dwconv(u, hist, w):
    ext = jnp.concatenate([hist.astype(u.dtype), u], axis=1)
    y = lax.conv_general_dilated(ext, w[:, None, :].astype(u.dtype), window_strides=(1,),
                                 padding='VALID', dimension_numbers=('NWC', 'WIO', 'NWC'),
                                 feature_group_count=u.shape[-1])
    return y, ext[:, ext.shape[1] - (w.shape[0] - 1):]


def ab_inputs(x, hist, norm_g, w_in, b_f, conv_w, q_g, k_g):
    bsz, t = x.shape[0], x.shape[1]
    h = rmsnorm(x, norm_g) @ w_in
    splits = [D_A, 2 * D_A, 3 * D_A, 3 * D_A + D_B, 3 * D_A + 2 * D_B, 3 * D_A + 3 * D_B]
    gb, gc, hv, q, k, v, fl = jnp.split(h, splits, axis=-1)
    yc, new_hist = causal_dwconv(gc * hv, hist, conv_w)
    y_a = gb * yc
    q = rmsnorm(q.reshape(bsz, t, N_HEADS_B, HEAD_DIM), q_g)
    k = rmsnorm(k.reshape(bsz, t, N_HEADS_B, HEAD_DIM), k_g)
    v = v.reshape(bsz, t, N_HEADS_B, HEAD_DIM)
    logf = jax.nn.log_sigmoid((fl + b_f).astype(jnp.float32))
    return y_a, q, k, v, logf, new_hist


def fox_attend(q, k, v, cq, ck, q_pos, k_pos):
    s = jnp.einsum('bqhd,bkhd->bhqk', q, k, preferred_element_type=jnp.float32) * (HEAD_DIM ** -0.5)
    s = s + (jnp.transpose(cq, (0, 2, 1))[:, :, :, None] - jnp.transpose(ck, (0, 2, 1))[:, :, None, :])
    s = jnp.where(k_pos[None, :] <= q_pos[:, None], s, -jnp.inf)
    p = jax.nn.softmax(s, axis=-1)
    return jnp.einsum('bhqk,bkhd->bqhd', p.astype(v.dtype), v)


def fox_prompt(q, k, v, logf):
    bsz, s_len, h, d = q.shape
    c = jnp.cumsum(logf, axis=1)
    nb = s_len // Q_BLOCK
    qb = q.reshape(bsz, nb, Q_BLOCK, h, d).transpose(1, 0, 2, 3, 4)
    cb = c.reshape(bsz, nb, Q_BLOCK, h).transpose(1, 0, 2, 3)
    k_pos = jnp.arange(s_len)

    def block(args):
        qi, ci, i = args
        q_pos = i * Q_BLOCK + jnp.arange(Q_BLOCK)
        return fox_attend(qi, k, v, ci, c, q_pos, k_pos)

    out = lax.map(block, (qb, cb, jnp.arange(nb)))
    return out.transpose(1, 0, 2, 3, 4).reshape(bsz, s_len, h * d)


def fox_sample(q, k_new, v_new, logf_new, k_pool, v_pool, logf_pool, page_table):
    bsz, t, h, d = q.shape
    p_len = page_table.shape[1] * PAGE_SIZE
    k_all = jnp.concatenate([k_pool[page_table].reshape(bsz, p_len, h, d).astype(k_new.dtype), k_new], axis=1)
    v_all = jnp.concatenate([v_pool[page_table].reshape(bsz, p_len, h, d).astype(v_new.dtype), v_new], axis=1)
    lf_all = jnp.concatenate([logf_pool[page_table].reshape(bsz, p_len, h).astype(jnp.float32), logf_new], axis=1)
    c = jnp.cumsum(lf_all, axis=1)
    q_pos = p_len + jnp.arange(t)
    k_pos = jnp.arange(p_len + t)
    out = fox_attend(q, k_all, v_all, c[:, p_len:], c, q_pos, k_pos)
    return out.reshape(bsz, t, h * d)


def conformer_conv(x, hist, norm_g, w_pw1, b_pw1, conv_w, conv_b, ln_g, ln_b, w_pw2):
    a, g = jnp.split(rmsnorm(x, norm_g) @ w_pw1 + b_pw1, 2, axis=-1)
    u = a * jax.nn.sigmoid(g)
    y, new_hist = causal_dwconv(u, hist, conv_w)
    y = layernorm(y + conv_b, ln_g, ln_b)
    return jax.nn.silu(y) @ w_pw2, new_hist


def conv_ffn(x, hist, norm_g, w_up, conv_w, w_down):
    gate, val = jnp.split(rmsnorm(x, norm_g) @ w_up, 2, axis=-1)
    gconv, new_hist = causal_dwconv(gate, hist, conv_w)
    return (jax.nn.silu(gconv) * val) @ w_down, new_hist


def setup_inputs(seed: int = 0) -> dict:
    key = jax.random.key(seed)
    ks = jax.random.split(key, 32)
    n_pages = PAST_LEN // PAGE_SIZE
    n_used = DEC_BATCH * n_pages
    n_pool = n_used + n_used // 4

    def nrm(k, shape, scale):
        return jax.random.normal(k, shape, jnp.float32) * scale

    return {
        'x_prompt': nrm(ks[0], (BATCH, SEQ, D_MODEL), 1.0),
        'x_sample': nrm(ks[1], (DEC_BATCH, DEC_SEQ, D_MODEL), 1.0),
        'cache_k': nrm(ks[2], (N_EVEN, n_pool, PAGE_SIZE, N_HEADS_B, HEAD_DIM), 1.0),
        'cache_v': nrm(ks[3], (N_EVEN, n_pool, PAGE_SIZE, N_HEADS_B, HEAD_DIM), 1.0),
        'cache_logf': jax.nn.log_sigmoid(3.0 + nrm(ks[4], (N_EVEN, n_pool, PAGE_SIZE, N_HEADS_B), 1.0)),
        'page_table': jax.random.permutation(ks[5], n_pool)[:n_used].reshape(DEC_BATCH, n_pages).astype(jnp.int32),
        'state_conv_a': nrm(ks[6], (N_EVEN, DEC_BATCH, CONV_A - 1, D_A), 1.0),
        'state_conv_c': nrm(ks[7], (N_ODD, DEC_BATCH, CONV_C - 1, D_C), 1.0),
        'state_ffn': nrm(ks[8], (DEPTH, DEC_BATCH, CONV_FFN - 1, D_FF), 1.0),
        'norm_mix0': 1.0 + nrm(ks[9], (N_EVEN, D_MODEL), 0.02),
        'w_in0': nrm(ks[10], (N_EVEN, D_MODEL, D_IN0), D_MODEL ** -0.5),
        'b_forget': 3.0 + nrm(ks[11], (N_EVEN, N_HEADS_B), 0.1),
        'conv_a_w': nrm(ks[12], (N_EVEN, CONV_A, D_A), CONV_A ** -0.5),
        'q_norm': 1.0 + nrm(ks[13], (N_EVEN, HEAD_DIM), 0.02),
        'k_norm': 1.0 + nrm(ks[14], (N_EVEN, HEAD_DIM), 0.02),
        'w_out0': nrm(ks[15], (N_EVEN, D_A + D_B, D_MODEL), (D_A + D_B) ** -0.5),
        'norm_mix1': 1.0 + nrm(ks[16], (N_ODD, D_MODEL), 0.02),
        'w_pw1': nrm(ks[17], (N_ODD, D_MODEL, 2 * D_C), D_MODEL ** -0.5),
        'b_pw1': nrm(ks[18], (N_ODD, 2 * D_C), 0.02),
        'conv_c_w': nrm(ks[19], (N_ODD, CONV_C, D_C), CONV_C ** -0.5),
        'conv_c_b': nrm(ks[20], (N_ODD, D_C), 0.02),
        'ln_c_g': 1.0 + nrm(ks[21], (N_ODD, D_C), 0.02),
        'ln_c_b': nrm(ks[22], (N_ODD, D_C), 0.02),
        'w_pw2': nrm(ks[23], (N_ODD, D_C, D_MODEL), D_C ** -0.5),
        'norm_ffn': 1.0 + nrm(ks[24], (DEPTH, D_MODEL), 0.02),
        'w_up': nrm(ks[25], (DEPTH, D_MODEL, 2 * D_FF), D_MODEL ** -0.5),
        'conv_ffn_w': nrm(ks[26], (DEPTH, CONV_FFN, D_FF), CONV_FFN ** -0.5),
        'w_down': nrm(ks[27], (DEPTH, D_FF, D_MODEL), D_FF ** -0.5),
    }


def reference(x_prompt, x_sample, cache_k, cache_v, cache_logf, page_table, state_conv_a,
              state_conv_c, state_ffn, norm_mix0, w_in0, b_forget, conv_a_w, q_norm, k_norm,
              w_out0, norm_mix1, w_pw1, b_pw1, conv_c_w, conv_c_b, ln_c_g, ln_c_b, w_pw2,
              norm_ffn, w_up, conv_ffn_w, w_down):
    yp, ys = x_prompt, x_sample
    pca, sca, pk, sk, pv, sv, plf, slf, pcc, scc, pff, sff = ([] for _ in range(12))
    for layer in range(DEPTH):
        i = layer // 2
        if layer % 2 == 0:
            zero_a = jnp.zeros((yp.shape[0], CONV_A - 1, D_A), yp.dtype)
            ya, q, k, v, lf, ha = ab_inputs(yp, zero_a, norm_mix0[i], w_in0[i], b_forget[i],
                                            conv_a_w[i], q_norm[i], k_norm[i])
            yb = fox_prompt(q, k, v, lf)
            yp = yp + jnp.concatenate([ya, yb.astype(ya.dtype)], axis=-1) @ w_out0[i]
            pca.append(ha); pk.append(k); pv.append(v); plf.append(lf)
            ya, q, k, v, lf, ha = ab_inputs(ys, state_conv_a[i], norm_mix0[i], w_in0[i], b_forget[i],
                                            conv_a_w[i], q_norm[i], k_norm[i])
            yb = fox_sample(q, k, v, lf, cache_k[i], cache_v[i], cache_logf[i], page_table)
            ys = ys + jnp.concatenate([ya, yb.astype(ya.dtype)], axis=-1) @ w_out0[i]
            sca.append(ha); sk.append(k); sv.append(v); slf.append(lf)
        else:
            zero_c = jnp.zeros((yp.shape[0], CONV_C - 1, D_C), yp.dtype)
            mp, hc = conformer_conv(yp, zero_c, norm_mix1[i], w_pw1[i], b_pw1[i], conv_c_w[i],
                                    conv_c_b[i], ln_c_g[i], ln_c_b[i], w_pw2[i])
            yp = yp + mp
            pcc.append(hc)
            ms, hc = conformer_conv(ys, state_conv_c[i], norm_mix1[i], w_pw1[i], b_pw1[i], conv_c_w[i],
                                    conv_c_b[i], ln_c_g[i], ln_c_b[i], w_pw2[i])
            ys = ys + ms
            scc.append(hc)
        zero_f = jnp.zeros((yp.shape[0], CONV_FFN - 1, D_FF), yp.dtype)
        fp, hf = conv_ffn(yp, zero_f, norm_ffn[layer], w_up[layer], conv_ffn_w[layer], w_down[layer])
        yp = yp + fp
        pff.append(hf)
        fs, hf = conv_ffn(ys, state_ffn[layer], norm_ffn[layer], w_up[layer], conv_ffn_w[layer], w_down[layer])
        ys = ys + fs
        sff.append(hf)
    return (yp, ys,
            jnp.stack(pca), jnp.stack(sca),
            jnp.stack(pk), jnp.stack(sk),
            jnp.stack(pv), jnp.stack(sv),
            jnp.stack(plf), jnp.stack(slf),
            jnp.stack(pcc), jnp.stack(scc),
            jnp.stack(pff), jnp.stack(sff))
```

```python
import functools

import jax
import jax.numpy as jnp
from jax import lax
from jax.experimental import pallas as pl
from jax.experimental.pallas import tpu as pltpu

D_MODEL = 1024
HEAD_DIM = 64
N_HEADS = 8
D_A = 512
D_B = 512
D_C = 1024
D_FF = 2816
CONV_A = 3
CONV_C = 31
CONV_FFN = 3
PAGE_SIZE = 128
EPS = 1e-6

V7X_LANES = 128
V7X_SUBLANES = 8
V7X_VMEM_BYTES = 64 * 1024 * 1024
NEG_BIG = -1e30

D_IN_PAD = 3 * D_A + 3 * D_B + V7X_LANES
BIAS_LANE = HEAD_DIM

PROMPT_TM = 256
PREP_TS = 1024
ATT_TQ = 512
PAGES_PER_STEP = 16

bf16 = jnp.bfloat16
f32 = jnp.float32


def _round_up(n, m):
    return (n + m - 1) // m * m


def _const_spec(shape):
    zeros = (0,) * len(shape)
    return pl.BlockSpec(shape, lambda *_: zeros, pipeline_mode=pl.Buffered(1))


def _params(semantics, vmem_bytes):
    return pltpu.CompilerParams(dimension_semantics=semantics,
                                vmem_limit_bytes=min(int(vmem_bytes), V7X_VMEM_BYTES - (4 << 20)))


def _rmsnorm(x, g):
    return x * lax.rsqrt(jnp.mean(x * x, axis=-1, keepdims=True) + EPS) * g


def _sigmoid(x):
    return 1.0 / (1.0 + jnp.exp(-x))


def _conv_rows(width, stride):
    h = (width - 1) * stride
    return _round_up(h, V7X_SUBLANES), h


def _causal_conv(ext_ref, u, w_ref, hist_ref, state_ref, *, width, stride, tm, tile, n_tiles):
    hp, h = _conv_rows(width, stride)

    @pl.when(tile == 0)
    def _():
        if hist_ref is None:
            ext_ref[0:hp, :] = jnp.zeros((hp, u.shape[1]), f32)
        else:
            ext_ref[hp - h:hp, :] = hist_ref[...]

    ext_ref[hp:hp + tm, :] = u
    y = None
    for k in range(width):
        start = hp - (width - 1 - k) * stride
        term = w_ref[k:k + 1, :] * ext_ref[start:start + tm, :]
        y = term if y is None else y + term
    state_ref[0] = ext_ref[hp + tm - h:hp + tm, :]
    if n_tiles > 1:
        ext_ref[0:hp, :] = ext_ref[tm:tm + hp, :]
    return y


def _in_proj_kernel(*refs, tm, stride, n_tiles, has_hist):
    if has_hist:
        (x_ref, g_ref, w_ref, bf_ref, cw_ref, qg_ref, kg_ref, bd_ref, hist_ref,
         ya_ref, q_ref, k_ref, v_ref, vb_ref, lf_ref, st_ref, ext_ref) = refs
    else:
        (x_ref, g_ref, w_ref, bf_ref, cw_ref, qg_ref, kg_ref, bd_ref,
         ya_ref, q_ref, k_ref, v_ref, vb_ref, lf_ref, st_ref, ext_ref) = refs
        hist_ref = None
    tile = pl.program_id(1)
    xn = _rmsnorm(x_ref[...], g_ref[...]).astype(bf16)
    h = jnp.dot(xn, w_ref[...], preferred_element_type=f32)
    gb = h[:, 0:D_A]
    u = h[:, D_A:2 * D_A] * h[:, 2 * D_A:3 * D_A]
    q = h[:, 3 * D_A:3 * D_A + D_B]
    k = h[:, 3 * D_A + D_B:3 * D_A + 2 * D_B]
    v = h[:, 3 * D_A + 2 * D_B:3 * D_A + 3 * D_B]
    fl = h[:, 3 * D_A + 3 * D_B:]

    yc = _causal_conv(ext_ref, u, cw_ref, hist_ref, st_ref, width=CONV_A, stride=stride, tm=tm,
                      tile=tile, n_tiles=n_tiles)
    ya_ref[...] = (gb * yc).astype(bf16)

    def head_norm(z, gain):
        ms = jnp.dot((z * z).astype(bf16), bd_ref[...], preferred_element_type=f32)
        return z * lax.rsqrt(ms + EPS) * gain

    q_ref[...] = head_norm(q, qg_ref[...]) * (HEAD_DIM ** -0.5)
    k_ref[...] = head_norm(k, kg_ref[...])
    v_ref[...] = v
    vb_ref[...] = v.astype(bf16)
    z = fl + bf_ref[...]
    logf = jnp.minimum(z, 0.0) - jnp.log(1.0 + jnp.exp(-jnp.abs(z)))
    lane = lax.broadcasted_iota(jnp.int32, logf.shape, 1)
    lf_ref[...] = jnp.where(lane < N_HEADS, logf, 0.0)


def _in_proj(x, g, w, bfv, cw, qg, kg, bd, hist, *, n_seq, n_tiles, tm, stride):
    n = x.shape[0]
    hp, h = _conv_rows(CONV_A, stride)
    row = lambda c: pl.BlockSpec((tm, c), lambda s, t: (s * n_tiles + t, 0))
    in_specs = [row(D_MODEL), _const_spec(g.shape), _const_spec(w.shape), _const_spec(bfv.shape),
                _const_spec(cw.shape), _const_spec(qg.shape), _const_spec(kg.shape), _const_spec(bd.shape)]
    args = [x, g, w, bfv, cw, qg, kg, bd]
    if hist is not None:
        in_specs.append(_const_spec(hist.shape))
        args.append(hist)
    out_shape = [jax.ShapeDtypeStruct((n, D_A), bf16),
                 jax.ShapeDtypeStruct((n, D_B), f32),
                 jax.ShapeDtypeStruct((n, D_B), f32),
                 jax.ShapeDtypeStruct((n, D_B), f32),
                 jax.ShapeDtypeStruct((n, D_B), bf16),
                 jax.ShapeDtypeStruct((n, V7X_LANES), f32),
                 jax.ShapeDtypeStruct((n_seq, h, D_A), f32)]
    out_specs = [row(D_A), row(D_B), row(D_B), row(D_B), row(D_B), row(V7X_LANES),
                 pl.BlockSpec((1, h, D_A), lambda s, t: (s, 0, 0))]
    vmem = 2 * w.size + 8 * tm * D_IN_PAD * 4 + (8 << 20)
    return pl.pallas_call(
        functools.partial(_in_proj_kernel, tm=tm, stride=stride, n_tiles=n_tiles, has_hist=hist is not None),
        grid=(n_seq, n_tiles), in_specs=in_specs, out_specs=out_specs, out_shape=out_shape,
        scratch_shapes=[pltpu.VMEM((hp + tm, D_A), f32)],
        compiler_params=_params(("arbitrary", "arbitrary"), vmem), name="in_proj")(*args)


def _split3(x):
    hi = x.astype(bf16).astype(f32)
    r = x - hi
    mid = r.astype(bf16).astype(f32)
    return hi, mid, r - mid


def _fox_prep_kernel(lf_ref, q_ref, k_ref, qt_ref, kt_ref, carry_ref, *, ts):
    @pl.when(pl.program_id(1) == 0)
    def _():
        carry_ref[...] = jnp.zeros(carry_ref.shape, f32)

    c = lf_ref[...]
    row = lax.broadcasted_iota(jnp.int32, c.shape, 0)
    shift = 1
    while shift < ts:
        c = c + jnp.where(row >= shift, pltpu.roll(c, shift, axis=0), 0.0)
        shift *= 2
    c = c + carry_ref[0:1, :]
    carry_ref[...] = jnp.broadcast_to(c[ts - 1:ts, :], carry_ref.shape)

    lane = lax.broadcasted_iota(jnp.int32, (ts, V7X_LANES), 1)
    ones = jnp.where((lane >= BIAS_LANE) & (lane < BIAS_LANE + 6), 1.0, 0.0)
    for hd in range(N_HEADS):
        hi, mid, lo = _split3(jnp.broadcast_to(c[:, hd:hd + 1], (ts, V7X_LANES)))
        pos = jnp.where(lane == BIAS_LANE, hi, jnp.where(lane == BIAS_LANE + 1, mid,
                        jnp.where(lane == BIAS_LANE + 2, lo, ones)))
        neg = jnp.where(lane == BIAS_LANE + 3, -hi, jnp.where(lane == BIAS_LANE + 4, -mid,
                        jnp.where(lane == BIAS_LANE + 5, -lo, ones)))
        g0 = (hd // 2) * V7X_LANES
        qs = q_ref[:, g0:g0 + V7X_LANES]
        ks = k_ref[:, g0:g0 + V7X_LANES]
        if hd % 2:
            qs = pltpu.roll(qs, HEAD_DIM, axis=1)
            ks = pltpu.roll(ks, HEAD_DIM, axis=1)
        qt_ref[0, hd] = jnp.where(lane < HEAD_DIM, qs, pos).astype(bf16)
        kt_ref[0, hd] = jnp.where(lane < HEAD_DIM, ks, neg).astype(bf16)


def _fox_prep(lf, q, k, *, n_seq, seq):
    ts = PREP_TS
    n_tiles = seq // ts
    row = lambda c: pl.BlockSpec((ts, c), lambda s, t: (s * n_tiles + t, 0))
    packed = pl.BlockSpec((1, N_HEADS, ts, V7X_LANES), lambda s, t: (s, 0, t, 0))
    shape = jax.ShapeDtypeStruct((n_seq, N_HEADS, seq, V7X_LANES), bf16)
    return pl.pallas_call(
        functools.partial(_fox_prep_kernel, ts=ts),
        grid=(n_seq, n_tiles), in_specs=[row(V7X_LANES), row(D_B), row(D_B)],
        out_specs=[packed, packed], out_shape=[shape, shape],
        scratch_shapes=[pltpu.VMEM((V7X_SUBLANES, V7X_LANES), f32)],
        compiler_params=_params(("arbitrary", "arbitrary"), 40 << 20), name="fox_prep")(lf, q, k)


def _fox_attn_kernel(qt_ref, kt_ref, vb_ref, o_ref, *, tq):
    qi = pl.program_id(2)
    q0 = qt_ref[0, 0]
    q1 = qt_ref[0, 1]
    lane = lax.broadcasted_iota(jnp.int32, (tq, V7X_LANES), 1)
    first = lane < HEAD_DIM
    contract_last = (((1,), (1,)), ((), ()))

    def step(ki, carry, diagonal):
        m0, l0, m1, l1, acc = carry
        ks = pl.multiple_of(ki * tq, tq)
        v2 = vb_ref[0, pl.ds(ks, tq), :]
        outs = []
        for q, hd, m, l in ((q0, 0, m0, l0), (q1, 1, m1, l1)):
            kk = kt_ref[0, hd, pl.ds(ks, tq), :]
            s = lax.dot_general(q, kk, contract_last, preferred_element_type=f32)
            if diagonal:
                r = lax.broadcasted_iota(jnp.int32, s.shape, 0)
                c = lax.broadcasted_iota(jnp.int32, s.shape, 1)
                s = jnp.where(c <= r, s, NEG_BIG)
            m_new = jnp.maximum(m, jnp.max(s, axis=-1, keepdims=True))
            alpha = jnp.exp(m - m_new)
            p = jnp.exp(s - m_new)
            l_new = alpha * l + jnp.sum(p, axis=-1, keepdims=True)
            o = jnp.dot(p.astype(bf16), v2, preferred_element_type=f32)
            outs.append((m_new, l_new, alpha, o))
        (m0, l0, a0, o0), (m1, l1, a1, o1) = outs
        acc = acc * jnp.where(first, a0, a1) + jnp.where(first, o0, o1)
        return m0, l0, m1, l1, acc

    col = lambda v: jnp.full((tq, 1), v, f32)
    init = (col(NEG_BIG), col(0.0), col(NEG_BIG), col(0.0), jnp.zeros((tq, V7X_LANES), f32))
    carry = lax.fori_loop(0, qi, lambda ki, c: step(ki, c, False), init)
    m0, l0, m1, l1, acc = step(qi, carry, True)
    o_ref[0] = (acc / jnp.where(first, l0, l1)).astype(o_ref.dtype)


def _fox_attn(qt, kt, vb, *, n_seq, seq):
    tq = ATT_TQ
    n_q = seq // tq
    pairs = N_HEADS // 2
    return pl.pallas_call(
        functools.partial(_fox_attn_kernel, tq=tq),
        grid=(n_seq, pairs, n_q),
        in_specs=[pl.BlockSpec((1, 2, tq, V7X_LANES), lambda b, p, i: (b, p, i, 0)),
                  pl.BlockSpec((1, 2, seq, V7X_LANES), lambda b, p, i: (b, p, 0, 0)),
                  pl.BlockSpec((1, seq, V7X_LANES), lambda b, p, i: (b, 0, p))],
        out_specs=pl.BlockSpec((1, tq, V7X_LANES), lambda b, p, i: (b, i, p)),
        out_shape=jax.ShapeDtypeStruct((n_seq, seq, D_B), bf16),
        compiler_params=_params(("arbitrary", "arbitrary", "arbitrary"), 40 << 20), name="fox_attn")(qt, kt, vb)


def _page_copies(pt_ref, ck_ref, cv_ref, cl_ref, kbuf, vbuf, lbuf, sems, step, slot, *, n_chunks, n_pages, g):
    b = step // n_chunks
    c = step % n_chunks
    base = b * n_pages + (n_chunks - 1 - c) * g
    copies = []
    for j in range(g):
        page = pt_ref[base + j]
        copies.append(pltpu.make_async_copy(ck_ref.at[page], kbuf.at[slot, j], sems.at[0, slot]))
        copies.append(pltpu.make_async_copy(cv_ref.at[page], vbuf.at[slot, j], sems.at[1, slot]))
        copies.append(pltpu.make_async_copy(cl_ref.at[page], lbuf.at[slot, j], sems.at[2, slot]))
    return copies


def _dot_f32(x, m01):
    out = None
    for piece in _split3(x):
        t = jnp.dot(piece.astype(bf16), m01, preferred_element_type=f32)
        out = t if out is None else out + t
    return out


def _sample_attn_kernel(pt_ref, q_ref, kn_ref, vn_ref, lfn_ref, ck_ref, cv_ref, cl_ref, o_ref,
                        kbuf, vbuf, lbuf, sems, m_ref, l_ref, acc_ref, run_ref, qbd_ref,
                        *, n_chunks, n_pages, g, n_tok):
    b = pl.program_id(0)
    c = pl.program_id(1)
    step = b * n_chunks + c
    n_steps = pl.num_programs(0) * n_chunks
    slot = step % 2
    copies = functools.partial(_page_copies, pt_ref, ck_ref, cv_ref, cl_ref, kbuf, vbuf, lbuf, sems,
                               n_chunks=n_chunks, n_pages=n_pages, g=g)
    rows = n_tok * N_HEADS
    contract_last = (((1,), (1,)), ((), ()))

    @pl.when(step == 0)
    def _():
        for cp in copies(step, slot):
            cp.start()

    @pl.when(step + 1 < n_steps)
    def _():
        for cp in copies(step + 1, 1 - slot):
            cp.start()

    @pl.when(c == 0)
    def _():
        lane = lax.broadcasted_iota(jnp.int32, (N_HEADS, D_B), 1)
        head = lax.broadcasted_iota(jnp.int32, (N_HEADS, D_B), 0)
        own = (lane // HEAD_DIM) == head
        q4 = q_ref[0]
        qbd = jnp.concatenate([jnp.where(own, jnp.broadcast_to(q4[t:t + 1, :], (N_HEADS, D_B)), 0.0)
                               for t in range(n_tok)], axis=0)
        qbd_ref[...] = qbd.astype(bf16)
        s = lax.dot_general(qbd.astype(bf16), kn_ref[0].astype(bf16), contract_last,
                            preferred_element_type=f32)
        lfn = lfn_ref[0]
        acc_rows = [lfn[0:1, :]]
        for t in range(1, V7X_SUBLANES):
            acc_rows.append(acc_rows[-1] + lfn[t:t + 1, :])
        dq = jnp.concatenate(acc_rows, axis=0)
        dq_t = dq.T[0:N_HEADS, 0:V7X_SUBLANES]
        bias = jnp.concatenate([dq_t] * n_tok, axis=0)
        r = lax.broadcasted_iota(jnp.int32, (rows, V7X_SUBLANES), 0) // N_HEADS
        t2 = lax.broadcasted_iota(jnp.int32, (rows, V7X_SUBLANES), 1)
        s = jnp.where(t2 <= r, s - bias, NEG_BIG)
        m = jnp.max(s, axis=-1, keepdims=True)
        p = jnp.exp(s - m)
        m_ref[...] = m
        l_ref[...] = jnp.sum(p, axis=-1, keepdims=True)
        acc_ref[...] = jnp.dot(p.astype(bf16), vn_ref[0].astype(bf16), preferred_element_type=f32)
        run_ref[...] = jnp.zeros(run_ref.shape, f32)

    for cp in copies(step, slot):
        cp.wait()

    kc = kbuf[slot].reshape(g * PAGE_SIZE, D_B).astype(bf16)
    vc = vbuf[slot].reshape(g * PAGE_SIZE, D_B).astype(bf16)
    s = lax.dot_general(qbd_ref[...], kc, contract_last, preferred_element_type=f32)

    lfc = lbuf[slot].reshape(g * N_HEADS, PAGE_SIZE)
    jj = lax.broadcasted_iota(jnp.int32, (PAGE_SIZE, PAGE_SIZE), 0)
    ss = lax.broadcasted_iota(jnp.int32, (PAGE_SIZE, PAGE_SIZE), 1)
    later = jnp.where(jj > ss, 1.0, 0.0).astype(bf16)
    within = _dot_f32(lfc, later)
    total = jnp.sum(lfc, axis=-1, keepdims=True)
    run = run_ref[:, 0:1]
    tiles = [None] * g
    for j in reversed(range(g)):
        tiles[j] = within[j * N_HEADS:(j + 1) * N_HEADS, :] + run
        run = run + total[j * N_HEADS:(j + 1) * N_HEADS, :]
    run_ref[...] = jnp.broadcast_to(run, run_ref.shape)
    bias = jnp.concatenate(tiles, axis=1)
    s = s + jnp.concatenate([bias] * n_tok, axis=0)

    m_old = m_ref[...]
    m_new = jnp.maximum(m_old, jnp.max(s, axis=-1, keepdims=True))
    alpha = jnp.exp(m_old - m_new)
    p = jnp.exp(s - m_new)
    m_ref[...] = m_new
    l_ref[...] = alpha * l_ref[...] + jnp.sum(p, axis=-1, keepdims=True)
    acc_ref[...] = alpha * acc_ref[...] + jnp.dot(p.astype(bf16), vc, preferred_element_type=f32)

    @pl.when(c == n_chunks - 1)
    def _():
        lane = lax.broadcasted_iota(jnp.int32, (N_HEADS, D_B), 1)
        head = lax.broadcasted_iota(jnp.int32, (N_HEADS, D_B), 0)
        own = (lane // HEAD_DIM) == head
        o = acc_ref[...] / l_ref[...]
        outs = [jnp.sum(jnp.where(own, o[t * N_HEADS:(t + 1) * N_HEADS, :], 0.0), axis=0, keepdims=True)
                for t in range(n_tok)]
        outs += [jnp.zeros((V7X_SUBLANES - n_tok, D_B), f32)]
        o_ref[0] = jnp.concatenate(outs, axis=0)


def _sample_attn(page_table, q, kn, vn, lfn, cache_k, cache_v, cache_lf_t, *, n_tok):
    n_batch, n_pages = page_table.shape
    g = PAGES_PER_STEP
    n_chunks = n_pages // g
    rows = n_tok * N_HEADS
    tok = lambda c: pl.BlockSpec((1, V7X_SUBLANES, c), lambda b, i, pt: (b, 0, 0))
    anyspec = pl.BlockSpec(memory_space=pl.ANY)
    grid_spec = pltpu.PrefetchScalarGridSpec(
        num_scalar_prefetch=1, grid=(n_batch, n_chunks),
        in_specs=[tok(D_B), tok(D_B), tok(D_B), tok(V7X_LANES), anyspec, anyspec, anyspec],
        out_specs=tok(D_B),
        scratch_shapes=[pltpu.VMEM((2, g, PAGE_SIZE, D_B), f32),
                        pltpu.VMEM((2, g, PAGE_SIZE, D_B), f32),
                        pltpu.VMEM((2, g, N_HEADS, PAGE_SIZE), f32),
                        pltpu.SemaphoreType.DMA((3, 2)),
                        pltpu.VMEM((rows, 1), f32), pltpu.VMEM((rows, 1), f32),
                        pltpu.VMEM((rows, D_B), f32), pltpu.VMEM((N_HEADS, V7X_LANES), f32),
                        pltpu.VMEM((rows, D_B), bf16)])
    vmem = 4 * g * PAGE_SIZE * D_B * 4 + 6 * g * PAGE_SIZE * D_B * 2 + (8 << 20)
    return pl.pallas_call(
        functools.partial(_sample_attn_kernel, n_chunks=n_chunks, n_pages=n_pages, g=g, n_tok=n_tok),
        grid_spec=grid_spec, out_shape=jax.ShapeDtypeStruct((n_batch, V7X_SUBLANES, D_B), f32),
        compiler_params=_params(("arbitrary", "arbitrary"), vmem), name="sample_attn")(
            page_table.reshape(-1), q, kn, vn, lfn, cache_k, cache_v, cache_lf_t)


def _out_proj_kernel(x_ref, ya_ref, yb_ref, wa_ref, wb_ref, o_ref):
    y = jnp.dot(ya_ref[...], wa_ref[...], preferred_element_type=f32)
    y = y + jnp.dot(yb_ref[...], wb_ref[...], preferred_element_type=f32)
    o_ref[...] = x_ref[...] + y


def _out_proj(x, ya, yb, wa, wb, *, tm):
    n = x.shape[0]
    row = lambda c: pl.BlockSpec((tm, c), lambda i: (i, 0))
    return pl.pallas_call(
        _out_proj_kernel, grid=(n // tm,),
        in_specs=[row(D_MODEL), row(D_A), row(D_B), _const_spec(wa.shape), _const_spec(wb.shape)],
        out_specs=row(D_MODEL), out_shape=jax.ShapeDtypeStruct((n, D_MODEL), f32),
        compiler_params=_params(("arbitrary",), 32 << 20), name="out_proj")(x, ya, yb, wa, wb)


def _ffn_kernel(*refs, tm, stride, n_tiles, has_hist):
    if has_hist:
        x_ref, g_ref, wu_ref, cw_ref, wd_ref, hist_ref, o_ref, st_ref, ext_ref = refs
    else:
        x_ref, g_ref, wu_ref, cw_ref, wd_ref, o_ref, st_ref, ext_ref = refs
        hist_ref = None
    x = x_ref[...]
    xn = _rmsnorm(x, g_ref[...]).astype(bf16)
    h = jnp.dot(xn, wu_ref[...], preferred_element_type=f32)
    gconv = _causal_conv(ext_ref, h[:, 0:D_FF], cw_ref, hist_ref, st_ref, width=CONV_FFN, stride=stride,
                         tm=tm, tile=pl.program_id(1), n_tiles=n_tiles)
    a = gconv * _sigmoid(gconv) * h[:, D_FF:]
    o_ref[...] = x + jnp.dot(a.astype(bf16), wd_ref[...], preferred_element_type=f32)


def _ffn(x, g, wu, cw, wd, hist, *, n_seq, n_tiles, tm, stride):
    n = x.shape[0]
    hp, h = _conv_rows(CONV_FFN, stride)
    row = pl.BlockSpec((tm, D_MODEL), lambda s, t: (s * n_tiles + t, 0))
    in_specs = [row, _const_spec(g.shape), _const_spec(wu.shape), _const_spec(cw.shape), _const_spec(wd.shape)]
    args = [x, g, wu, cw, wd]
    if hist is not None:
        in_specs.append(_const_spec(hist.shape))
        args.append(hist)
    vmem = 2 * (wu.size + wd.size) + 8 * tm * D_FF * 4 + (8 << 20)
    return pl.pallas_call(
        functools.partial(_ffn_kernel, tm=tm, stride=stride, n_tiles=n_tiles, has_hist=hist is not None),
        grid=(n_seq, n_tiles), in_specs=in_specs,
        out_specs=[row, pl.BlockSpec((1, h, D_FF), lambda s, t: (s, 0, 0))],
        out_shape=[jax.ShapeDtypeStruct((n, D_MODEL), f32), jax.ShapeDtypeStruct((n_seq, h, D_FF), f32)],
        scratch_shapes=[pltpu.VMEM((hp + tm, D_FF), f32)],
        compiler_params=_params(("arbitrary", "arbitrary"), vmem), name="conv_ffn")(*args)


def _conformer_kernel(*refs, tm, stride, n_tiles, has_hist):
    if has_hist:
        (x_ref, g_ref, w1_ref, b1_ref, cw_ref, cb_ref, lg_ref, lb_ref, w2_ref, hist_ref,
         o_ref, st_ref, ext_ref) = refs
    else:
        (x_ref, g_ref, w1_ref, b1_ref, cw_ref, cb_ref, lg_ref, lb_ref, w2_ref,
         o_ref, st_ref, ext_ref) = refs
        hist_ref = None
    x = x_ref[...]
    xn = _rmsnorm(x, g_ref[...]).astype(bf16)
    h = jnp.dot(xn, w1_ref[...], preferred_element_type=f32) + b1_ref[...]
    u = h[:, 0:D_C] * _sigmoid(h[:, D_C:])
    y = _causal_conv(ext_ref, u, cw_ref, hist_ref, st_ref, width=CONV_C, stride=stride, tm=tm,
                     tile=pl.program_id(1), n_tiles=n_tiles) + cb_ref[...]
    mu = jnp.mean(y, axis=-1, keepdims=True)
    yc = y - mu
    yn = yc * lax.rsqrt(jnp.mean(yc * yc, axis=-1, keepdims=True) + EPS) * lg_ref[...] + lb_ref[...]
    a = yn * _sigmoid(yn)
    o_ref[...] = x + jnp.dot(a.astype(bf16), w2_ref[...], preferred_element_type=f32)


def _conformer(x, g, w1, b1, cw, cb, lg, lb, w2, hist, *, n_seq, n_tiles, tm, stride):
    n = x.shape[0]
    hp, h = _conv_rows(CONV_C, stride)
    row = pl.BlockSpec((tm, D_MODEL), lambda s, t: (s * n_tiles + t, 0))
    consts = [g, w1, b1, cw, cb, lg, lb, w2] + ([hist] if hist is not None else [])
    vmem = 2 * (w1.size + w2.size) + 8 * (hp + tm) * 2 * D_C * 4 + (8 << 20)
    return pl.pallas_call(
        functools.partial(_conformer_kernel, tm=tm, stride=stride, n_tiles=n_tiles, has_hist=hist is not None),
        grid=(n_seq, n_tiles), in_specs=[row] + [_const_spec(a.shape) for a in consts],
        out_specs=[row, pl.BlockSpec((1, h, D_C), lambda s, t: (s, 0, 0))],
        out_shape=[jax.ShapeDtypeStruct((n, D_MODEL), f32), jax.ShapeDtypeStruct((n_seq, h, D_C), f32)],
        scratch_shapes=[pltpu.VMEM((hp + tm, D_C), f32)],
        compiler_params=_params(("arbitrary", "arbitrary"), vmem), name="conformer")(x, *consts)


def _time_major(a):
    return jnp.swapaxes(a, 0, 1).reshape(-1, a.shape[-1])


def _batch_major(a, n_batch):
    return jnp.swapaxes(a.reshape(-1, n_batch, a.shape[-1]), 0, 1)


def _pad_tokens(a):
    return jnp.pad(a, ((0, 0), (0, V7X_SUBLANES - a.shape[1]), (0, 0)))


def kernel(x_prompt, x_sample, cache_k, cache_v, cache_logf, page_table, state_conv_a, state_conv_c, state_ffn, norm_mix0, w_in0, b_forget, conv_a_w, q_norm, k_norm, w_out0, norm_mix1, w_pw1, b_pw1, conv_c_w, conv_c_b, ln_c_g, ln_c_b, w_pw2, norm_ffn, w_up, conv_ffn_w, w_down):
    n_batch, seq, _ = x_prompt.shape
    dec_batch, dec_seq, _ = x_sample.shape
    n_pool = cache_k.shape[1]
    p_tiles = seq // PROMPT_TM
    prompt = dict(n_seq=n_batch, n_tiles=p_tiles, tm=PROMPT_TM, stride=1)
    sample = dict(n_seq=1, n_tiles=1, tm=dec_batch * dec_seq, stride=dec_batch)
    vec = lambda a: a.reshape(1, -1)

    yp = x_prompt.reshape(n_batch * seq, D_MODEL)
    ys = _time_major(x_sample)

    w_in = jnp.pad(w_in0[0], ((0, 0), (0, D_IN_PAD - w_in0.shape[2]))).astype(bf16)
    b_f = jnp.pad(b_forget[0], (0, V7X_LANES - N_HEADS)).reshape(1, V7X_LANES)
    q_g = vec(jnp.tile(q_norm[0], N_HEADS))
    k_g = vec(jnp.tile(k_norm[0], N_HEADS))
    lane_head = jnp.arange(D_B) // HEAD_DIM
    head_mean = jnp.where(lane_head[:, None] == lane_head[None, :], 1.0 / HEAD_DIM, 0.0).astype(bf16)
    w_oa = w_out0[0, :D_A].astype(bf16)
    w_ob = w_out0[0, D_A:].astype(bf16)
    proj = (vec(norm_mix0[0]), w_in, b_f, conv_a_w[0], q_g, k_g, head_mean)

    ya, q, k, v, vb, lf, pca = _in_proj(yp, *proj, None, **prompt)
    qt, kt = _fox_prep(lf, q, k, n_seq=n_batch, seq=seq)
    yb = _fox_attn(qt, kt, vb.reshape(n_batch, seq, D_B), n_seq=n_batch, seq=seq)
    yp = _out_proj(yp, ya, yb.reshape(n_batch * seq, D_B), w_oa, w_ob, tm=PROMPT_TM)
    p_k = k.reshape(1, n_batch, seq, N_HEADS, HEAD_DIM)
    p_v = v.reshape(1, n_batch, seq, N_HEADS, HEAD_DIM)
    p_logf = lf[:, :N_HEADS].reshape(1, n_batch, seq, N_HEADS)

    ya_s, q_s, k_s, v_s, _, lf_s, sca = _in_proj(ys, *proj, _time_major(state_conv_a[0]), **sample)
    to_batch = lambda a: _batch_major(a, dec_batch)
    yb_s = _sample_attn(page_table, _pad_tokens(to_batch(q_s)), _pad_tokens(to_batch(k_s)),
                        _pad_tokens(to_batch(v_s)), _pad_tokens(to_batch(lf_s)),
                        cache_k[0].reshape(n_pool, PAGE_SIZE, D_B), cache_v[0].reshape(n_pool, PAGE_SIZE, D_B),
                        jnp.swapaxes(cache_logf[0], 1, 2), n_tok=dec_seq)
    yb_s = _time_major(yb_s[:, :dec_seq]).astype(bf16)
    ys = _out_proj(ys, ya_s, yb_s, w_oa, w_ob, tm=dec_batch * dec_seq)
    s_k = to_batch(k_s).reshape(1, dec_batch, dec_seq, N_HEADS, HEAD_DIM)
    s_v = to_batch(v_s).reshape(1, dec_batch, dec_seq, N_HEADS, HEAD_DIM)
    s_logf = to_batch(lf_s)[:, :, :N_HEADS].reshape(1, dec_batch, dec_seq, N_HEADS)
    s_conv_a = to_batch(sca[0])[None]

    p_ffn, s_ffn = [], []

    def ffn_pair(yp, ys, layer):
        wu = w_up[layer].astype(bf16)
        wd = w_down[layer].astype(bf16)
        yp, pst = _ffn(yp, vec(norm_ffn[layer]), wu, conv_ffn_w[layer], wd, None, **prompt)
        ys, sst = _ffn(ys, vec(norm_ffn[layer]), wu, conv_ffn_w[layer], wd, _time_major(state_ffn[layer]), **sample)
        p_ffn.append(pst)
        s_ffn.append(to_batch(sst[0]))
        return yp, ys

    yp, ys = ffn_pair(yp, ys, 0)

    conf = (vec(norm_mix1[0]), w_pw1[0].astype(bf16), vec(b_pw1[0]), conv_c_w[0], vec(conv_c_b[0]),
            vec(ln_c_g[0]), vec(ln_c_b[0]), w_pw2[0].astype(bf16))
    yp, pcc = _conformer(yp, *conf, None, **prompt)
    ys, scc = _conformer(ys, *conf, _time_major(state_conv_c[0]), **sample)
    yp, ys = ffn_pair(yp, ys, 1)

    return (yp.reshape(n_batch, seq, D_MODEL), to_batch(ys),
            pca[None], s_conv_a,
            p_k, s_k, p_v, s_v, p_logf, s_logf,
            pcc[None], to_batch(scc[0])[None],
            jnp.stack(p_ffn), jnp.stack(s_ffn))
```

```python
import functools

import jax
import jax.numpy as jnp
from jax import lax
from jax.experimental import pallas as pl
from jax.experimental.pallas import tpu as pltpu

D_MODEL = 1024
HEAD_DIM = 64
N_HEADS = 8
D_A = 512
D_B = 512
D_C = 1024
D_FF = 2816
CONV_A = 3
CONV_C = 31
CONV_FFN = 3
PAGE_SIZE = 128
EPS = 1e-6

V7X_LANES = 128
V7X_SUBLANES = 8
V7X_VMEM_BYTES = 64 * 1024 * 1024
NEG_BIG = -1e30
LOG2E = 1.4426950408889634

D_IN_PAD = 3 * D_A + 3 * D_B + V7X_LANES
BIAS_LANE = HEAD_DIM
TAIL_ROWS = V7X_SUBLANES

PROMPT_TM = 256
WIDE_TM = 512
FFN_CHUNK = 256
CONV_CHUNK = 256
PREP_TS = 1024
PAGES_PER_STEP = 16

bf16 = jnp.bfloat16
f32 = jnp.float32


def _round_up(n, m):
    return (n + m - 1) // m * m


def _const_spec(shape):
    zeros = (0,) * len(shape)
    return pl.BlockSpec(shape, lambda *_: zeros, pipeline_mode=pl.Buffered(1))


def _params(semantics, vmem_bytes):
    return pltpu.CompilerParams(dimension_semantics=semantics,
                                vmem_limit_bytes=min(int(vmem_bytes), V7X_VMEM_BYTES - (4 << 20)))


def _rmsnorm(x, g):
    return x * lax.rsqrt(jnp.mean(x * x, axis=-1, keepdims=True) + EPS) * g


def _sigmoid(x):
    return 1.0 / (1.0 + jnp.exp(-x))


def _conv_rows(width, stride):
    h = (width - 1) * stride
    return _round_up(h, V7X_SUBLANES), h


def _conv_init(ext_ref, hist_ref, *, width, stride, tile):
    hp, h = _conv_rows(width, stride)

    @pl.when(tile == 0)
    def _():
        if hist_ref is None:
            ext_ref[0:hp, :] = jnp.zeros((hp, ext_ref.shape[1]), f32)
        else:
            ext_ref[hp - h:hp, :] = hist_ref[...]


def _causal_conv(ext_ref, u, w_ref, state_ref, *, width, stride, tm, n_tiles, cols=slice(None), group_ref=None):
    hp, h = _conv_rows(width, stride)
    ext_ref[hp:hp + tm, cols] = u
    starts = [hp - (width - 1 - k) * stride for k in range(width)]
    y = None
    if group_ref is None:
        for k, start in enumerate(starts):
            term = w_ref[k:k + 1, cols] * ext_ref[start:start + tm, cols]
            y = term if y is None else y + term
    else:
        for off in range(V7X_SUBLANES):
            taps = [k for k, start in enumerate(starts) if start % V7X_SUBLANES == off]
            if not taps:
                continue
            rows = tm if off == 0 else tm + V7X_SUBLANES
            part = None
            for k in taps:
                base = starts[k] - off
                term = w_ref[k:k + 1, cols] * ext_ref[base:base + rows, cols]
                part = term if part is None else part + term
            if off:
                group_ref[off - 1, 0:rows, :] = part
                part = group_ref[off - 1, off:off + tm, :]
            y = part if y is None else y + part
    state_ref[0, :, cols] = ext_ref[hp + tm - h:hp + tm, cols]
    if n_tiles > 1:
        ext_ref[0:hp, cols] = ext_ref[tm:tm + hp, cols]
    return y


def _in_proj_kernel(*refs, tm, stride, n_tiles, has_hist):
    if has_hist:
        (x_ref, g_ref, w_ref, bf_ref, cw_ref, qg_ref, kg_ref, bd_ref, hist_ref,
         ya_ref, qt_ref, k_ref, kt_ref, vt_ref, vbt_ref, lf_ref, lft_ref, st_ref, ext_ref) = refs
    else:
        (x_ref, g_ref, w_ref, bf_ref, cw_ref, qg_ref, kg_ref, bd_ref,
         ya_ref, qt_ref, k_ref, kt_ref, vt_ref, vbt_ref, lf_ref, lft_ref, st_ref, ext_ref) = refs
        hist_ref = None
    tile = pl.program_id(1)
    xn = _rmsnorm(x_ref[...], g_ref[...]).astype(bf16)
    h = jnp.dot(xn, w_ref[...], preferred_element_type=f32)
    gb = h[:, 0:D_A]
    u = h[:, D_A:2 * D_A] * h[:, 2 * D_A:3 * D_A]
    q = h[:, 3 * D_A:3 * D_A + D_B]
    k = h[:, 3 * D_A + D_B:3 * D_A + 2 * D_B]
    v = h[:, 3 * D_A + 2 * D_B:3 * D_A + 3 * D_B]
    fl = h[:, 3 * D_A + 3 * D_B:]

    _conv_init(ext_ref, hist_ref, width=CONV_A, stride=stride, tile=tile)
    yc = _causal_conv(ext_ref, u, cw_ref, st_ref, width=CONV_A, stride=stride, tm=tm, n_tiles=n_tiles)
    ya_ref[...] = (gb * yc).astype(bf16)

    def head_norm(z, gain):
        ms = jnp.dot((z * z).astype(bf16), bd_ref[...], preferred_element_type=f32)
        return z * lax.rsqrt(ms + EPS) * gain

    qt_ref[0] = (head_norm(q, qg_ref[...]) * (HEAD_DIM ** -0.5)).T
    kn = head_norm(k, kg_ref[...])
    k_ref[...] = kn
    kt_ref[0] = kn.T
    vt = v.T
    vt_ref[0] = vt
    vbt_ref[0, 0] = vt.astype(bf16)
    z = fl + bf_ref[...]
    logf = jnp.minimum(z, 0.0) - jnp.log(1.0 + jnp.exp(-jnp.abs(z)))
    lane = lax.broadcasted_iota(jnp.int32, logf.shape, 1)
    logf = jnp.where(lane < N_HEADS, logf, 0.0)
    lf_ref[...] = logf
    lft_ref[0] = logf.T[0:N_HEADS, :]


def _in_proj(x, g, w, bfv, cw, qg, kg, bd, hist, *, n_seq, n_tiles, tm, stride):
    n = x.shape[0]
    seq = n_tiles * tm
    hp, h = _conv_rows(CONV_A, stride)
    row = lambda c: pl.BlockSpec((tm, c), lambda s, t: (s * n_tiles + t, 0))
    col = lambda r: pl.BlockSpec((1, r, tm), lambda s, t: (s, 0, t))
    in_specs = [row(D_MODEL), _const_spec(g.shape), _const_spec(w.shape), _const_spec(bfv.shape),
                _const_spec(cw.shape), _const_spec(qg.shape), _const_spec(kg.shape), _const_spec(bd.shape)]
    args = [x, g, w, bfv, cw, qg, kg, bd]
    if hist is not None:
        in_specs.append(_const_spec(hist.shape))
        args.append(hist)
    out_shape = [jax.ShapeDtypeStruct((n, D_A), bf16),
                 jax.ShapeDtypeStruct((n_seq, D_B, seq), f32),
                 jax.ShapeDtypeStruct((n, D_B), f32),
                 jax.ShapeDtypeStruct((n_seq, D_B, seq), f32),
                 jax.ShapeDtypeStruct((n_seq, D_B, seq), f32),
                 jax.ShapeDtypeStruct((n_seq, n_tiles, D_B, tm), bf16),
                 jax.ShapeDtypeStruct((n, V7X_LANES), f32),
                 jax.ShapeDtypeStruct((n_seq, N_HEADS, seq), f32),
                 jax.ShapeDtypeStruct((n_seq, h, D_A), f32)]
    out_specs = [row(D_A), col(D_B), row(D_B), col(D_B), col(D_B),
                 pl.BlockSpec((1, 1, D_B, tm), lambda s, t: (s, t, 0, 0)),
                 row(V7X_LANES), col(N_HEADS),
                 pl.BlockSpec((1, h, D_A), lambda s, t: (s, 0, 0))]
    vmem = 2 * w.size + 10 * tm * D_IN_PAD * 4 + (8 << 20)
    return pl.pallas_call(
        functools.partial(_in_proj_kernel, tm=tm, stride=stride, n_tiles=n_tiles, has_hist=hist is not None),
        grid=(n_seq, n_tiles), in_specs=in_specs, out_specs=out_specs, out_shape=out_shape,
        scratch_shapes=[pltpu.VMEM((hp + tm, D_A), f32)],
        compiler_params=_params(("arbitrary", "arbitrary"), vmem), name="in_proj")(*args)


def _split3(x):
    hi = x.astype(bf16).astype(f32)
    r = x - hi
    mid = r.astype(bf16).astype(f32)
    return hi, mid, r - mid


def _fox_prep_kernel(lf_ref, k_ref, kt_ref, qtail_ref, carry_ref, *, ts):
    @pl.when(pl.program_id(1) == 0)
    def _():
        carry_ref[...] = jnp.zeros(carry_ref.shape, f32)

    c = lf_ref[...]
    row = lax.broadcasted_iota(jnp.int32, c.shape, 0)
    shift = 1
    while shift < ts:
        c = c + jnp.where(row >= shift, pltpu.roll(c, shift, axis=0), 0.0)
        shift *= 2
    c = c + carry_ref[0:1, :]
    carry_ref[...] = jnp.broadcast_to(c[ts - 1:ts, :], carry_ref.shape)
    c = c * LOG2E
    c_t = c.T

    lane = lax.broadcasted_iota(jnp.int32, (ts, V7X_LANES), 1)
    ones = jnp.where((lane >= BIAS_LANE) & (lane < BIAS_LANE + 3), 1.0, 0.0)
    one_row = jnp.ones((1, ts), f32)
    zero_row = jnp.zeros((1, ts), f32)
    for hd in range(N_HEADS):
        hi, mid, lo = _split3(jnp.broadcast_to(c[:, hd:hd + 1], (ts, V7X_LANES)))
        neg = jnp.where(lane == BIAS_LANE + 3, -hi, jnp.where(lane == BIAS_LANE + 4, -mid,
                        jnp.where(lane == BIAS_LANE + 5, -lo, ones)))
        g0 = (hd // 2) * V7X_LANES
        ks = k_ref[:, g0:g0 + V7X_LANES]
        if hd % 2:
            ks = pltpu.roll(ks, HEAD_DIM, axis=1)
        kt_ref[0, hd] = jnp.where(lane < HEAD_DIM, ks, neg).astype(bf16)
        hi_t, mid_t, lo_t = _split3(c_t[hd:hd + 1, :])
        qtail_ref[0, hd] = jnp.concatenate([hi_t, mid_t, lo_t, one_row, one_row, one_row, zero_row, zero_row],
                                           axis=0)


def _fox_prep(lf, k, *, n_seq, seq):
    ts = PREP_TS
    n_tiles = seq // ts
    row = lambda c: pl.BlockSpec((ts, c), lambda s, t: (s * n_tiles + t, 0))
    return pl.pallas_call(
        functools.partial(_fox_prep_kernel, ts=ts),
        grid=(n_seq, n_tiles), in_specs=[row(V7X_LANES), row(D_B)],
        out_specs=[pl.BlockSpec((1, N_HEADS, ts, V7X_LANES), lambda s, t: (s, 0, t, 0)),
                   pl.BlockSpec((1, N_HEADS, TAIL_ROWS, ts), lambda s, t: (s, 0, 0, t))],
        out_shape=[jax.ShapeDtypeStruct((n_seq, N_HEADS, seq, V7X_LANES), bf16),
                   jax.ShapeDtypeStruct((n_seq, N_HEADS, TAIL_ROWS, seq), f32)],
        scratch_shapes=[pltpu.VMEM((V7X_SUBLANES, V7X_LANES), f32)],
        compiler_params=_params(("arbitrary", "arbitrary"), 40 << 20), name="fox_prep")(lf, k)


def _fox_attn_kernel(qt_ref, qtail_ref, kt_ref, vbt_ref, o_ref, qa_ref, m_ref, l_ref, acc_ref, sa_ref, sb_ref, *, tq):
    qi = pl.program_id(1)
    pad = jnp.zeros((V7X_LANES - HEAD_DIM - TAIL_ROWS, tq), f32)
    for hd in range(N_HEADS):
        qa = jnp.concatenate([qt_ref[0, hd * HEAD_DIM:(hd + 1) * HEAD_DIM, :] * LOG2E, qtail_ref[0, hd], pad], axis=0)
        qa_ref[hd] = qa.astype(bf16)
    m_ref[...] = jnp.full(m_ref.shape, NEG_BIG, f32)
    l_ref[...] = jnp.zeros(l_ref.shape, f32)
    acc_ref[...] = jnp.zeros(acc_ref.shape, f32)

    def scores_into(s_ref, j):
        ks = pl.multiple_of(j * tq, tq)
        for hd in range(N_HEADS):
            s_ref[hd] = jnp.dot(kt_ref[0, hd, pl.ds(ks, tq), :], qa_ref[hd],
                                preferred_element_type=f32)

    def consume(s_ref, j, diagonal):
        probs, alphas = [], []
        for hd in range(N_HEADS):
            s = s_ref[hd]
            if diagonal:
                key = lax.broadcasted_iota(jnp.int32, s.shape, 0)
                qry = lax.broadcasted_iota(jnp.int32, s.shape, 1)
                s = jnp.where(key <= qry, s, NEG_BIG)
            m_old = m_ref[hd]
            m_new = jnp.maximum(m_old, jnp.max(s, axis=0, keepdims=True))
            alpha = jnp.exp2(m_old - m_new)
            p = jnp.exp2(s - m_new)
            m_ref[hd] = m_new
            l_ref[hd] = alpha * l_ref[hd] + jnp.sum(p, axis=0, keepdims=True)
            probs.append(p.astype(bf16))
            alphas.append(alpha)
        for hd in range(N_HEADS):
            vv = vbt_ref[0, j, hd * HEAD_DIM:(hd + 1) * HEAD_DIM, :]
            acc_ref[hd] = alphas[hd] * acc_ref[hd] + jnp.dot(vv, probs[hd], preferred_element_type=f32)

    scores_into(sa_ref, 0)

    def body(i, carry):
        j = 2 * i
        scores_into(sb_ref, j + 1)
        consume(sa_ref, j, False)
        scores_into(sa_ref, j + 2)
        consume(sb_ref, j + 1, False)
        return carry

    lax.fori_loop(0, qi // 2, body, 0)

    @pl.when(qi % 2 == 0)
    def _():
        consume(sa_ref, qi, True)

    @pl.when(qi % 2 == 1)
    def _():
        scores_into(sb_ref, qi)
        consume(sa_ref, qi - 1, False)
        consume(sb_ref, qi, True)

    o = jnp.concatenate([acc_ref[hd] / l_ref[hd] for hd in range(N_HEADS)], axis=0)
    o_ref[0] = o.T.astype(o_ref.dtype)


def _fox_attn(qt, qtail, kt, vbt, *, n_seq, seq):
    tq = PROMPT_TM
    return pl.pallas_call(
        functools.partial(_fox_attn_kernel, tq=tq),
        grid=(n_seq, seq // tq),
        in_specs=[pl.BlockSpec((1, D_B, tq), lambda b, i: (b, 0, i)),
                  pl.BlockSpec((1, N_HEADS, TAIL_ROWS, tq), lambda b, i: (b, 0, 0, i)),
                  pl.BlockSpec((1, N_HEADS, seq, V7X_LANES), lambda b, i: (b, 0, 0, 0)),
                  pl.BlockSpec((1, seq // tq, D_B, tq), lambda b, i: (b, 0, 0, 0))],
        out_specs=pl.BlockSpec((1, tq, D_B), lambda b, i: (b, i, 0)),
        out_shape=jax.ShapeDtypeStruct((n_seq, seq, D_B), bf16),
        scratch_shapes=[pltpu.VMEM((N_HEADS, V7X_LANES, tq), bf16), pltpu.VMEM((N_HEADS, 1, tq), f32),
                        pltpu.VMEM((N_HEADS, 1, tq), f32), pltpu.VMEM((N_HEADS, HEAD_DIM, tq), f32),
                        pltpu.VMEM((N_HEADS, tq, tq), f32), pltpu.VMEM((N_HEADS, tq, tq), f32)],
        compiler_params=_params(("arbitrary", "arbitrary"), 48 << 20), name="fox_attn")(
            qt, qtail, kt, vbt)


def _page_copies(pt_ref, ck_ref, cv_ref, cl_ref, kbuf, vbuf, lbuf, sems, step, slot, *, n_chunks, n_pages, g):
    b = step // n_chunks
    c = step % n_chunks
    base = b * n_pages + (n_chunks - 1 - c) * g
    copies = []
    for j in range(g):
        page = pt_ref[base + j]
        lanes = pl.ds(j * PAGE_SIZE, PAGE_SIZE)
        copies.append(pltpu.make_async_copy(ck_ref.at[page], kbuf.at[slot, :, lanes], sems.at[0, slot]))
        copies.append(pltpu.make_async_copy(cv_ref.at[page], vbuf.at[slot, :, lanes], sems.at[1, slot]))
        copies.append(pltpu.make_async_copy(cl_ref.at[page], lbuf.at[slot, :, lanes], sems.at[2, slot]))
    return copies


def _dot_f32(x, m01):
    out = None
    for piece in _split3(x):
        t = jnp.dot(piece.astype(bf16), m01, preferred_element_type=f32)
        out = t if out is None else out + t
    return out


def _sample_attn_kernel(pt_ref, q_ref, kn_ref, vn_ref, lfn_ref, ck_ref, cv_ref, cl_ref, o_ref,
                        kbuf, vbuf, lbuf, sems, m_ref, l_ref, acc_ref, run_ref, qbd_ref,
                        *, n_chunks, n_pages, g, n_tok):
    b = pl.program_id(0)
    c = pl.program_id(1)
    step = b * n_chunks + c
    n_steps = pl.num_programs(0) * n_chunks
    slot = step % 2
    copies = functools.partial(_page_copies, pt_ref, ck_ref, cv_ref, cl_ref, kbuf, vbuf, lbuf, sems,
                               n_chunks=n_chunks, n_pages=n_pages, g=g)
    rows = n_tok * N_HEADS
    contract_last = (((1,), (1,)), ((), ()))

    @pl.when(step == 0)
    def _():
        for cp in copies(step, slot):
            cp.start()

    @pl.when(step + 1 < n_steps)
    def _():
        for cp in copies(step + 1, 1 - slot):
            cp.start()

    @pl.when(c == 0)
    def _():
        lane = lax.broadcasted_iota(jnp.int32, (N_HEADS, D_B), 1)
        head = lax.broadcasted_iota(jnp.int32, (N_HEADS, D_B), 0)
        own = (lane // HEAD_DIM) == head
        q4 = q_ref[0]
        qbd = jnp.concatenate([jnp.where(own, jnp.broadcast_to(q4[t:t + 1, :], (N_HEADS, D_B)), 0.0)
                               for t in range(n_tok)], axis=0)
        qbd_ref[...] = qbd.astype(bf16)
        s = lax.dot_general(qbd.astype(bf16), kn_ref[0].astype(bf16), contract_last,
                            preferred_element_type=f32)
        lfn = lfn_ref[0]
        acc_rows = [lfn[0:1, :]]
        for t in range(1, V7X_SUBLANES):
            acc_rows.append(acc_rows[-1] + lfn[t:t + 1, :])
        dq = jnp.concatenate(acc_rows, axis=0)
        dq_t = dq.T[0:N_HEADS, 0:V7X_SUBLANES]
        bias = jnp.concatenate([dq_t] * n_tok, axis=0)
        r = lax.broadcasted_iota(jnp.int32, (rows, V7X_SUBLANES), 0) // N_HEADS
        t2 = lax.broadcasted_iota(jnp.int32, (rows, V7X_SUBLANES), 1)
        s = jnp.where(t2 <= r, s - bias, NEG_BIG)
        m = jnp.max(s, axis=-1, keepdims=True)
        p = jnp.exp(s - m)
        m_ref[...] = m
        l_ref[...] = jnp.sum(p, axis=-1, keepdims=True)
        acc_ref[...] = jnp.dot(p.astype(bf16), vn_ref[0].astype(bf16), preferred_element_type=f32)
        run_ref[...] = jnp.zeros(run_ref.shape, f32)

    for cp in copies(step, slot):
        cp.wait()

    s = jnp.dot(qbd_ref[...], kbuf[slot].astype(bf16), preferred_element_type=f32)

    lfc = lbuf[slot]
    lfp = jnp.concatenate([lfc[:, j * PAGE_SIZE:(j + 1) * PAGE_SIZE] for j in range(g)], axis=0)
    jj = lax.broadcasted_iota(jnp.int32, (PAGE_SIZE, PAGE_SIZE), 0)
    ss = lax.broadcasted_iota(jnp.int32, (PAGE_SIZE, PAGE_SIZE), 1)
    later = jnp.where(jj > ss, 1.0, 0.0).astype(bf16)
    within = _dot_f32(lfp, later)
    total = jnp.sum(lfp, axis=-1, keepdims=True)
    run = run_ref[:, 0:1]
    tiles = [None] * g
    for j in reversed(range(g)):
        tiles[j] = within[j * N_HEADS:(j + 1) * N_HEADS, :] + run
        run = run + total[j * N_HEADS:(j + 1) * N_HEADS, :]
    run_ref[...] = jnp.broadcast_to(run, run_ref.shape)
    bias = jnp.concatenate(tiles, axis=1)
    s = s + jnp.concatenate([bias] * n_tok, axis=0)

    m_old = m_ref[...]
    m_new = jnp.maximum(m_old, jnp.max(s, axis=-1, keepdims=True))
    alpha = jnp.exp(m_old - m_new)
    p = jnp.exp(s - m_new)
    m_ref[...] = m_new
    l_ref[...] = alpha * l_ref[...] + jnp.sum(p, axis=-1, keepdims=True)
    pv = lax.dot_general(p.astype(bf16), vbuf[slot].astype(bf16), contract_last, preferred_element_type=f32)
    acc_ref[...] = alpha * acc_ref[...] + pv

    @pl.when(c == n_chunks - 1)
    def _():
        lane = lax.broadcasted_iota(jnp.int32, (N_HEADS, D_B), 1)
        head = lax.broadcasted_iota(jnp.int32, (N_HEADS, D_B), 0)
        own = (lane // HEAD_DIM) == head
        o = acc_ref[...] / l_ref[...]
        outs = [jnp.sum(jnp.where(own, o[t * N_HEADS:(t + 1) * N_HEADS, :], 0.0), axis=0, keepdims=True)
                for t in range(n_tok)]
        outs += [jnp.zeros((V7X_SUBLANES - n_tok, D_B), f32)]
        o_ref[0] = jnp.concatenate(outs, axis=0)


def _sample_attn(page_table, q, kn, vn, lfn, cache_kt, cache_vt, cache_lt, *, n_tok):
    n_batch, n_pages = page_table.shape
    g = PAGES_PER_STEP
    n_chunks = n_pages // g
    rows = n_tok * N_HEADS
    tok = lambda c: pl.BlockSpec((1, V7X_SUBLANES, c), lambda b, i, pt: (b, 0, 0))
    anyspec = pl.BlockSpec(memory_space=pl.ANY)
    grid_spec = pltpu.PrefetchScalarGridSpec(
        num_scalar_prefetch=1, grid=(n_batch, n_chunks),
        in_specs=[tok(D_B), tok(D_B), tok(D_B), tok(V7X_LANES), anyspec, anyspec, anyspec],
        out_specs=tok(D_B),
        scratch_shapes=[pltpu.VMEM((2, D_B, g * PAGE_SIZE), f32),
                        pltpu.VMEM((2, D_B, g * PAGE_SIZE), f32),
                        pltpu.VMEM((2, N_HEADS, g * PAGE_SIZE), f32),
                        pltpu.SemaphoreType.DMA((3, 2)),
                        pltpu.VMEM((rows, 1), f32), pltpu.VMEM((rows, 1), f32),
                        pltpu.VMEM((rows, D_B), f32), pltpu.VMEM((N_HEADS, V7X_LANES), f32),
                        pltpu.VMEM((rows, D_B), bf16)])
    vmem = 4 * g * PAGE_SIZE * D_B * 4 + 6 * g * PAGE_SIZE * D_B * 2 + (8 << 20)
    return pl.pallas_call(
        functools.partial(_sample_attn_kernel, n_chunks=n_chunks, n_pages=n_pages, g=g, n_tok=n_tok),
        grid_spec=grid_spec, out_shape=jax.ShapeDtypeStruct((n_batch, V7X_SUBLANES, D_B), f32),
        compiler_params=_params(("arbitrary", "arbitrary"), vmem), name="sample_attn")(
            page_table.reshape(-1), q, kn, vn, lfn, cache_kt, cache_vt, cache_lt)


def _out_proj_kernel(x_ref, ya_ref, yb_ref, wa_ref, wb_ref, o_ref):
    y = jnp.dot(ya_ref[...], wa_ref[...], preferred_element_type=f32)
    y = y + jnp.dot(yb_ref[...], wb_ref[...], preferred_element_type=f32)
    o_ref[...] = x_ref[...] + y


def _out_proj(x, ya, yb, wa, wb, *, tm):
    n = x.shape[0]
    row = lambda c: pl.BlockSpec((tm, c), lambda i: (i, 0))
    return pl.pallas_call(
        _out_proj_kernel, grid=(n // tm,),
        in_specs=[row(D_MODEL), row(D_A), row(D_B), _const_spec(wa.shape), _const_spec(wb.shape)],
        out_specs=row(D_MODEL), out_shape=jax.ShapeDtypeStruct((n, D_MODEL), f32),
        compiler_params=_params(("arbitrary",), 32 << 20), name="out_proj")(x, ya, yb, wa, wb)


def _ffn_kernel(*refs, tm, stride, n_tiles, has_hist):
    if has_hist:
        x_ref, g_ref, wu_ref, cw_ref, wd_ref, hist_ref, o_ref, st_ref, ext_ref, a_ref = refs
    else:
        x_ref, g_ref, wu_ref, cw_ref, wd_ref, o_ref, st_ref, ext_ref, a_ref = refs
        hist_ref = None
    xn = _rmsnorm(x_ref[...], g_ref[...]).astype(bf16)
    _conv_init(ext_ref, hist_ref, width=CONV_FFN, stride=stride, tile=pl.program_id(1))
    for c0 in range(0, D_FF, FFN_CHUNK):
        cols = slice(c0, c0 + FFN_CHUNK)
        gate = jnp.dot(xn, wu_ref[:, c0:c0 + FFN_CHUNK], preferred_element_type=f32)
        val = jnp.dot(xn, wu_ref[:, D_FF + c0:D_FF + c0 + FFN_CHUNK], preferred_element_type=f32)
        gconv = _causal_conv(ext_ref, gate, cw_ref, st_ref, width=CONV_FFN, stride=stride, tm=tm,
                             n_tiles=n_tiles, cols=cols)
        a_ref[:, cols] = (gconv * _sigmoid(gconv) * val).astype(bf16)
    o_ref[...] = x_ref[...] + jnp.dot(a_ref[...], wd_ref[...], preferred_element_type=f32)


def _ffn(x, g, wu, cw, wd, hist, *, n_seq, n_tiles, tm, stride):
    n = x.shape[0]
    hp, h = _conv_rows(CONV_FFN, stride)
    row = pl.BlockSpec((tm, D_MODEL), lambda s, t: (s * n_tiles + t, 0))
    in_specs = [row, _const_spec(g.shape), _const_spec(wu.shape), _const_spec(cw.shape), _const_spec(wd.shape)]
    args = [x, g, wu, cw, wd]
    if hist is not None:
        in_specs.append(_const_spec(hist.shape))
        args.append(hist)
    vmem = 2 * (wu.size + wd.size) + 16 * tm * D_MODEL + (hp + tm) * D_FF * 4 + tm * D_FF * 2 + (12 << 20)
    return pl.pallas_call(
        functools.partial(_ffn_kernel, tm=tm, stride=stride, n_tiles=n_tiles, has_hist=hist is not None),
        grid=(n_seq, n_tiles), in_specs=in_specs,
        out_specs=[row, pl.BlockSpec((1, h, D_FF), lambda s, t: (s, 0, 0))],
        out_shape=[jax.ShapeDtypeStruct((n, D_MODEL), f32), jax.ShapeDtypeStruct((n_seq, h, D_FF), f32)],
        scratch_shapes=[pltpu.VMEM((hp + tm, D_FF), f32), pltpu.VMEM((tm, D_FF), bf16)],
        compiler_params=_params(("arbitrary", "arbitrary"), vmem), name="conv_ffn")(*args)


def _conformer_kernel(*refs, tm, stride, n_tiles, has_hist, grouped):
    refs = list(refs)
    group_ref = refs.pop() if grouped else None
    if has_hist:
        (x_ref, g_ref, w1_ref, b1_ref, cw_ref, cb_ref, lg_ref, lb_ref, w2_ref, hist_ref,
         o_ref, st_ref, ext_ref) = refs
    else:
        (x_ref, g_ref, w1_ref, b1_ref, cw_ref, cb_ref, lg_ref, lb_ref, w2_ref,
         o_ref, st_ref, ext_ref) = refs
        hist_ref = None
    xn = _rmsnorm(x_ref[...], g_ref[...]).astype(bf16)
    _conv_init(ext_ref, hist_ref, width=CONV_C, stride=stride, tile=pl.program_id(1))
    parts = []
    for j, c0 in enumerate(range(0, D_C, CONV_CHUNK)):
        cols = slice(c0, c0 + CONV_CHUNK)
        gcols = slice(D_C + c0, D_C + c0 + CONV_CHUNK)
        a = jnp.dot(xn, w1_ref[:, cols], preferred_element_type=f32) + b1_ref[:, cols]
        g = jnp.dot(xn, w1_ref[:, gcols], preferred_element_type=f32) + b1_ref[:, gcols]
        y = _causal_conv(ext_ref, a * _sigmoid(g), cw_ref, st_ref, width=CONV_C, stride=stride, tm=tm,
                         n_tiles=n_tiles, cols=cols, group_ref=group_ref.at[j] if grouped else None)
        parts.append(y + cb_ref[:, cols])
    y = jnp.concatenate(parts, axis=1)
    mu = jnp.mean(y, axis=-1, keepdims=True)
    yc = y - mu
    yn = yc * lax.rsqrt(jnp.mean(yc * yc, axis=-1, keepdims=True) + EPS) * lg_ref[...] + lb_ref[...]
    a = yn * _sigmoid(yn)
    o_ref[...] = x_ref[...] + jnp.dot(a.astype(bf16), w2_ref[...], preferred_element_type=f32)


def _conformer(x, g, w1, b1, cw, cb, lg, lb, w2, hist, *, n_seq, n_tiles, tm, stride):
    n = x.shape[0]
    hp, h = _conv_rows(CONV_C, stride)
    grouped = stride % V7X_SUBLANES != 0
    row = pl.BlockSpec((tm, D_MODEL), lambda s, t: (s * n_tiles + t, 0))
    consts = [g, w1, b1, cw, cb, lg, lb, w2] + ([hist] if hist is not None else [])
    scratch = [pltpu.VMEM((hp + tm, D_C), f32)]
    group_bytes = 0
    if grouped:
        group_shape = (D_C // CONV_CHUNK, V7X_SUBLANES - 1, tm + V7X_SUBLANES, CONV_CHUNK)
        scratch.append(pltpu.VMEM(group_shape, f32))
        group_bytes = 4 * group_shape[0] * group_shape[1] * group_shape[2] * group_shape[3]
    vmem = 2 * (w1.size + w2.size) + 16 * tm * D_MODEL + 6 * (hp + tm) * D_C * 4 + group_bytes + (8 << 20)
    return pl.pallas_call(
        functools.partial(_conformer_kernel, tm=tm, stride=stride, n_tiles=n_tiles, has_hist=hist is not None,
                          grouped=grouped),
        grid=(n_seq, n_tiles), in_specs=[row] + [_const_spec(a.shape) for a in consts],
        out_specs=[row, pl.BlockSpec((1, h, D_C), lambda s, t: (s, 0, 0))],
        out_shape=[jax.ShapeDtypeStruct((n, D_MODEL), f32), jax.ShapeDtypeStruct((n_seq, h, D_C), f32)],
        scratch_shapes=scratch,
        compiler_params=_params(("arbitrary", "arbitrary"), vmem), name="conformer")(x, *consts)


def _time_major(a):
    return jnp.swapaxes(a, 0, 1).reshape(-1, a.shape[-1])


def _batch_major(a, n_batch):
    return jnp.swapaxes(a.reshape(-1, n_batch, a.shape[-1]), 0, 1)


def _pad_tokens(a):
    return jnp.pad(a, ((0, 0), (0, V7X_SUBLANES - a.shape[1]), (0, 0)))


def _heads_last(a_t, n_seq, seq):
    return jnp.transpose(a_t.reshape(n_seq, N_HEADS, HEAD_DIM, seq), (0, 3, 1, 2))


def kernel(x_prompt, x_sample, cache_k, cache_v, cache_logf, page_table, state_conv_a, state_conv_c, state_ffn, norm_mix0, w_in0, b_forget, conv_a_w, q_norm, k_norm, w_out0, norm_mix1, w_pw1, b_pw1, conv_c_w, conv_c_b, ln_c_g, ln_c_b, w_pw2, norm_ffn, w_up, conv_ffn_w, w_down):
    n_batch, seq, _ = x_prompt.shape
    dec_batch, dec_seq, _ = x_sample.shape
    n_pool = cache_k.shape[1]
    p_tiles = seq // PROMPT_TM
    prompt = dict(n_seq=n_batch, n_tiles=p_tiles, tm=PROMPT_TM, stride=1)
    wide = dict(n_seq=n_batch, n_tiles=seq // WIDE_TM, tm=WIDE_TM, stride=1)
    sample = dict(n_seq=1, n_tiles=1, tm=dec_batch * dec_seq, stride=dec_batch)
    vec = lambda a: a.reshape(1, -1)

    yp = x_prompt.reshape(n_batch * seq, D_MODEL)
    ys = _time_major(x_sample)

    w_in = jnp.pad(w_in0[0], ((0, 0), (0, D_IN_PAD - w_in0.shape[2]))).astype(bf16)
    b_f = jnp.pad(b_forget[0], (0, V7X_LANES - N_HEADS)).reshape(1, V7X_LANES)
    q_g = vec(jnp.tile(q_norm[0], N_HEADS))
    k_g = vec(jnp.tile(k_norm[0], N_HEADS))
    lane_head = jnp.arange(D_B) // HEAD_DIM
    head_mean = jnp.where(lane_head[:, None] == lane_head[None, :], 1.0 / HEAD_DIM, 0.0).astype(bf16)
    w_oa = w_out0[0, :D_A].astype(bf16)
    w_ob = w_out0[0, D_A:].astype(bf16)
    proj = (vec(norm_mix0[0]), w_in, b_f, conv_a_w[0], q_g, k_g, head_mean)

    ya, q_t, k, k_t, v_t, vb_t, lf, lf_t, pca = _in_proj(yp, *proj, None, **prompt)
    kt, qtail = _fox_prep(lf, k, n_seq=n_batch, seq=seq)
    yb = _fox_attn(q_t, qtail, kt, vb_t, n_seq=n_batch, seq=seq)
    yp = _out_proj(yp, ya, yb.reshape(n_batch * seq, D_B), w_oa, w_ob, tm=WIDE_TM)
    p_k = _heads_last(k_t, n_batch, seq)[None]
    p_v = _heads_last(v_t, n_batch, seq)[None]
    p_logf = jnp.swapaxes(lf_t, 1, 2)[None]

    ya_s, q_ts, k_s, _, v_ts, _, lf_s, _, sca = _in_proj(ys, *proj, _time_major(state_conv_a[0]), **sample)
    to_batch = lambda a: _batch_major(a, dec_batch)
    q_s = to_batch(q_ts[0].T)
    v_s = to_batch(v_ts[0].T)
    yb_s = _sample_attn(page_table, _pad_tokens(q_s), _pad_tokens(to_batch(k_s)), _pad_tokens(v_s),
                        _pad_tokens(to_batch(lf_s)),
                        jnp.transpose(cache_k[0], (0, 2, 3, 1)).reshape(n_pool, D_B, PAGE_SIZE),
                        jnp.transpose(cache_v[0], (0, 2, 3, 1)).reshape(n_pool, D_B, PAGE_SIZE),
                        jnp.swapaxes(cache_logf[0], 1, 2), n_tok=dec_seq)
    yb_s = _time_major(yb_s[:, :dec_seq]).astype(bf16)
    ys = _out_proj(ys, ya_s, yb_s, w_oa, w_ob, tm=dec_batch * dec_seq)
    s_k = to_batch(k_s).reshape(1, dec_batch, dec_seq, N_HEADS, HEAD_DIM)
    s_v = v_s.reshape(1, dec_batch, dec_seq, N_HEADS, HEAD_DIM)
    s_logf = to_batch(lf_s)[:, :, :N_HEADS].reshape(1, dec_batch, dec_seq, N_HEADS)
    s_conv_a = to_batch(sca[0])[None]

    p_ffn, s_ffn = [], []

    def ffn_pair(yp, ys, layer):
        wu = w_up[layer].astype(bf16)
        wd = w_down[layer].astype(bf16)
        yp, pst = _ffn(yp, vec(norm_ffn[layer]), wu, conv_ffn_w[layer], wd, None, **wide)
        ys, sst = _ffn(ys, vec(norm_ffn[layer]), wu, conv_ffn_w[layer], wd, _time_major(state_ffn[layer]), **sample)
        p_ffn.append(pst)
        s_ffn.append(to_batch(sst[0]))
        return yp, ys

    yp, ys = ffn_pair(yp, ys, 0)

    conf = (vec(norm_mix1[0]), w_pw1[0].astype(bf16), vec(b_pw1[0]), conv_c_w[0], vec(conv_c_b[0]),
            vec(ln_c_g[0]), vec(ln_c_b[0]), w_pw2[0].astype(bf16))
    yp, pcc = _conformer(yp, *conf, None, **wide)
    ys, scc = _conformer(ys, *conf, _time_major(state_conv_c[0]), **sample)
    yp, ys = ffn_pair(yp, ys, 1)

    return (yp.reshape(n_batch, seq, D_MODEL), to_batch(ys),
            pca[None], s_conv_a,
            p_k, s_k, p_v, s_v, p_logf, s_logf,
            pcc[None], to_batch(scc[0])[None],
            jnp.stack(p_ffn), jnp.stack(s_ffn))
```

```python
import functools

import jax
import jax.numpy as jnp
from jax import lax
from jax.experimental import pallas as pl
from jax.experimental.pallas import tpu as pltpu

D_MODEL = 1024
HEAD_DIM = 64
N_HEADS = 8
D_A = 512
D_B = 512
D_C = 1024
D_FF = 2816
CONV_A = 3
CONV_C = 31
CONV_FFN = 3
PAGE_SIZE = 128
EPS = 1e-6

V7X_LANES = 128
V7X_SUBLANES = 8
V7X_VMEM_BYTES = 64 * 1024 * 1024
NEG_BIG = -1e30
LOG2E = 1.4426950408889634

D_IN_PAD = 3 * D_A + 3 * D_B + V7X_LANES
BIAS_LANE = HEAD_DIM
TAIL_ROWS = V7X_SUBLANES

PROMPT_TM = 256
WIDE_TM = 512
FFN_CHUNK = 256
CONV_CHUNK = 256
PREP_TS = 1024
PAGES_PER_STEP = 16

bf16 = jnp.bfloat16
f32 = jnp.float32


def _round_up(n, m):
    return (n + m - 1) // m * m


def _const_spec(shape):
    zeros = (0,) * len(shape)
    return pl.BlockSpec(shape, lambda *_: zeros, pipeline_mode=pl.Buffered(1))


def _params(semantics, vmem_bytes):
    return pltpu.CompilerParams(dimension_semantics=semantics,
                                vmem_limit_bytes=min(int(vmem_bytes), V7X_VMEM_BYTES - (4 << 20)))


def _rmsnorm(x, g):
    return x * lax.rsqrt(jnp.mean(x * x, axis=-1, keepdims=True) + EPS) * g


def _sigmoid(x):
    return 1.0 / (1.0 + jnp.exp(-x))


def _conv_rows(width, stride):
    h = (width - 1) * stride
    return _round_up(h, V7X_SUBLANES), h


def _conv_init(ext_ref, hist_ref, *, width, stride, tile):
    hp, h = _conv_rows(width, stride)

    @pl.when(tile == 0)
    def _():
        if hist_ref is None:
            ext_ref[0:hp, :] = jnp.zeros((hp, ext_ref.shape[1]), f32)
        else:
            ext_ref[hp - h:hp, :] = hist_ref[...]


def _causal_conv(ext_ref, u, w_ref, state_ref, *, width, stride, tm, n_tiles, cols=slice(None), group_ref=None):
    hp, h = _conv_rows(width, stride)
    ext_ref[hp:hp + tm, cols] = u
    starts = [hp - (width - 1 - k) * stride for k in range(width)]
    y = None
    if group_ref is None:
        for k, start in enumerate(starts):
            term = w_ref[k:k + 1, cols] * ext_ref[start:start + tm, cols]
            y = term if y is None else y + term
    else:
        for off in range(V7X_SUBLANES):
            taps = [k for k, start in enumerate(starts) if start % V7X_SUBLANES == off]
            if not taps:
                continue
            rows = tm if off == 0 else tm + V7X_SUBLANES
            part = None
            for k in taps:
                base = starts[k] - off
                term = w_ref[k:k + 1, cols] * ext_ref[base:base + rows, cols]
                part = term if part is None else part + term
            if off:
                group_ref[off - 1, 0:rows, :] = part
                part = group_ref[off - 1, off:off + tm, :]
            y = part if y is None else y + part
    state_ref[0, :, cols] = ext_ref[hp + tm - h:hp + tm, cols]
    if n_tiles > 1:
        ext_ref[0:hp, cols] = ext_ref[tm:tm + hp, cols]
    return y


def _in_proj_kernel(*refs, tm, stride, n_tiles, has_hist, midway):
    if has_hist:
        (x_ref, g_ref, w_ref, bf_ref, cw_ref, qg_ref, kg_ref, bd_ref, hist_ref,
         ya_ref, qt_ref, k_ref, kt_ref, vt_ref, vbt_ref, lf_ref, lft_ref, st_ref, ext_ref) = refs
    else:
        (x_ref, g_ref, w_ref, bf_ref, cw_ref, qg_ref, kg_ref, bd_ref,
         ya_ref, qt_ref, k_ref, kt_ref, vt_ref, vbt_ref, lf_ref, lft_ref, st_ref, ext_ref) = refs
        hist_ref = None
    tile = pl.program_id(1)
    xn = _rmsnorm(x_ref[...], g_ref[...]).astype(bf16)
    _conv_init(ext_ref, hist_ref, width=CONV_A, stride=stride, tile=tile)
    h = jnp.dot(xn, w_ref[:, 0:3 * D_A], preferred_element_type=f32)
    u = h[:, D_A:2 * D_A] * h[:, 2 * D_A:3 * D_A]
    yc = _causal_conv(ext_ref, u, cw_ref, st_ref, width=CONV_A, stride=stride, tm=tm, n_tiles=n_tiles)
    ya_ref[...] = (h[:, 0:D_A] * yc).astype(bf16)
    midway()
    h = jnp.dot(xn, w_ref[:, 3 * D_A:], preferred_element_type=f32)
    q = h[:, 0:D_B]
    k = h[:, D_B:2 * D_B]
    v = h[:, 2 * D_B:3 * D_B]
    fl = h[:, 3 * D_B:]

    def head_norm(z, gain):
        ms = jnp.dot((z * z).astype(bf16), bd_ref[...], preferred_element_type=f32)
        return z * lax.rsqrt(ms + EPS) * gain

    qt_ref[0] = (head_norm(q, qg_ref[...]) * (HEAD_DIM ** -0.5)).T
    kn = head_norm(k, kg_ref[...])
    k_ref[...] = kn
    kt_ref[0] = kn.T
    vt = v.T
    vt_ref[0] = vt
    vbt_ref[0, 0] = vt.astype(bf16)
    z = fl + bf_ref[...]
    logf = jnp.minimum(z, 0.0) - jnp.log(1.0 + jnp.exp(-jnp.abs(z)))
    lane = lax.broadcasted_iota(jnp.int32, logf.shape, 1)
    logf = jnp.where(lane < N_HEADS, logf, 0.0)
    lf_ref[...] = logf
    lft_ref[0] = logf.T[0:N_HEADS, :]


def _in_proj(x, g, w, bfv, cw, qg, kg, bd, hist, *, n_seq, n_tiles, tm, stride, side=None):
    n = x.shape[0]
    seq = n_tiles * tm
    hp, h = _conv_rows(CONV_A, stride)
    row = lambda c: pl.BlockSpec((tm, c), lambda s, t, *_: (s * n_tiles + t, 0))
    col = lambda r: pl.BlockSpec((1, r, tm), lambda s, t, *_: (s, 0, t))
    in_specs = [row(D_MODEL), _const_spec(g.shape), _const_spec(w.shape), _const_spec(bfv.shape),
                _const_spec(cw.shape), _const_spec(qg.shape), _const_spec(kg.shape), _const_spec(bd.shape)]
    args = [x, g, w, bfv, cw, qg, kg, bd]
    if hist is not None:
        in_specs.append(_const_spec(hist.shape))
        args.append(hist)
    out_shape = [jax.ShapeDtypeStruct((n, D_A), bf16),
                 jax.ShapeDtypeStruct((n_seq, D_B, seq), f32),
                 jax.ShapeDtypeStruct((n, D_B), f32),
                 jax.ShapeDtypeStruct((n_seq, D_B, seq), f32),
                 jax.ShapeDtypeStruct((n_seq, D_B, seq), f32),
                 jax.ShapeDtypeStruct((n_seq, n_tiles, D_B, tm), bf16),
                 jax.ShapeDtypeStruct((n, V7X_LANES), f32),
                 jax.ShapeDtypeStruct((n_seq, N_HEADS, seq), f32),
                 jax.ShapeDtypeStruct((n_seq, h, D_A), f32)]
    out_specs = [row(D_A), col(D_B), row(D_B), col(D_B), col(D_B),
                 pl.BlockSpec((1, 1, D_B, tm), lambda s, t, *_: (s, t, 0, 0)),
                 row(V7X_LANES), col(N_HEADS),
                 pl.BlockSpec((1, h, D_A), lambda s, t, *_: (s, 0, 0))]
    vmem = 2 * w.size + 10 * tm * D_IN_PAD * 4 + (8 << 20)
    return _host_call(_in_proj_kernel, "in_proj", grid=(n_seq, n_tiles), in_specs=in_specs, args=args,
                      out_specs=out_specs, out_shape=out_shape, scratch=[pltpu.VMEM((hp + tm, D_A), f32)],
                      vmem=vmem, static=dict(tm=tm, stride=stride, has_hist=hist is not None), side=side)


def _split3(x):
    hi = x.astype(bf16).astype(f32)
    r = x - hi
    mid = r.astype(bf16).astype(f32)
    return hi, mid, r - mid


def _fox_prep_kernel(lf_ref, k_ref, kt_ref, qtail_ref, carry_ref, *, ts):
    @pl.when(pl.program_id(1) == 0)
    def _():
        carry_ref[...] = jnp.zeros(carry_ref.shape, f32)

    c = lf_ref[...]
    row = lax.broadcasted_iota(jnp.int32, c.shape, 0)
    shift = 1
    while shift < ts:
        c = c + jnp.where(row >= shift, pltpu.roll(c, shift, axis=0), 0.0)
        shift *= 2
    c = c + carry_ref[0:1, :]
    carry_ref[...] = jnp.broadcast_to(c[ts - 1:ts, :], carry_ref.shape)
    c = c * LOG2E
    c_t = c.T

    lane = lax.broadcasted_iota(jnp.int32, (ts, V7X_LANES), 1)
    ones = jnp.where((lane >= BIAS_LANE) & (lane < BIAS_LANE + 3), 1.0, 0.0)
    one_row = jnp.ones((1, ts), f32)
    zero_row = jnp.zeros((1, ts), f32)
    for hd in range(N_HEADS):
        hi, mid, lo = _split3(jnp.broadcast_to(c[:, hd:hd + 1], (ts, V7X_LANES)))
        neg = jnp.where(lane == BIAS_LANE + 3, -hi, jnp.where(lane == BIAS_LANE + 4, -mid,
                        jnp.where(lane == BIAS_LANE + 5, -lo, ones)))
        g0 = (hd // 2) * V7X_LANES
        ks = k_ref[:, g0:g0 + V7X_LANES]
        if hd % 2:
            ks = pltpu.roll(ks, HEAD_DIM, axis=1)
        kt_ref[0, hd] = jnp.where(lane < HEAD_DIM, ks, neg).astype(bf16)
        hi_t, mid_t, lo_t = _split3(c_t[hd:hd + 1, :])
        qtail_ref[0, hd] = jnp.concatenate([hi_t, mid_t, lo_t, one_row, one_row, one_row, zero_row, zero_row],
                                           axis=0)


def _fox_prep(lf, k, *, n_seq, seq):
    ts = PREP_TS
    n_tiles = seq // ts
    row = lambda c: pl.BlockSpec((ts, c), lambda s, t, *_: (s * n_tiles + t, 0))
    return pl.pallas_call(
        functools.partial(_fox_prep_kernel, ts=ts),
        grid=(n_seq, n_tiles), in_specs=[row(V7X_LANES), row(D_B)],
        out_specs=[pl.BlockSpec((1, N_HEADS, ts, V7X_LANES), lambda s, t: (s, 0, t, 0)),
                   pl.BlockSpec((1, N_HEADS, TAIL_ROWS, ts), lambda s, t: (s, 0, 0, t))],
        out_shape=[jax.ShapeDtypeStruct((n_seq, N_HEADS, seq, V7X_LANES), bf16),
                   jax.ShapeDtypeStruct((n_seq, N_HEADS, TAIL_ROWS, seq), f32)],
        scratch_shapes=[pltpu.VMEM((V7X_SUBLANES, V7X_LANES), f32)],
        compiler_params=_params(("arbitrary", "arbitrary"), 40 << 20), name="fox_prep")(lf, k)


def _fox_attn_kernel(qt_ref, qtail_ref, kt_ref, vbt_ref, o_ref, qa_ref, m_ref, l_ref, acc_ref, sa_ref, sb_ref, *, tq):
    qi = pl.program_id(1)
    pad = jnp.zeros((V7X_LANES - HEAD_DIM - TAIL_ROWS, tq), f32)
    for hd in range(N_HEADS):
        qa = jnp.concatenate([qt_ref[0, hd * HEAD_DIM:(hd + 1) * HEAD_DIM, :] * LOG2E, qtail_ref[0, hd], pad], axis=0)
        qa_ref[hd] = qa.astype(bf16)
    m_ref[...] = jnp.full(m_ref.shape, NEG_BIG, f32)
    l_ref[...] = jnp.zeros(l_ref.shape, f32)
    acc_ref[...] = jnp.zeros(acc_ref.shape, f32)

    def scores_into(s_ref, j):
        ks = pl.multiple_of(j * tq, tq)
        for hd in range(N_HEADS):
            s_ref[hd] = jnp.dot(kt_ref[0, hd, pl.ds(ks, tq), :], qa_ref[hd],
                                preferred_element_type=f32)

    def consume(s_ref, j, diagonal):
        probs, alphas = [], []
        for hd in range(N_HEADS):
            s = s_ref[hd]
            if diagonal:
                key = lax.broadcasted_iota(jnp.int32, s.shape, 0)
                qry = lax.broadcasted_iota(jnp.int32, s.shape, 1)
                s = jnp.where(key <= qry, s, NEG_BIG)
            m_old = m_ref[hd]
            m_new = jnp.maximum(m_old, jnp.max(s, axis=0, keepdims=True))
            alpha = jnp.exp2(m_old - m_new)
            p = jnp.exp2(s - m_new)
            m_ref[hd] = m_new
            l_ref[hd] = alpha * l_ref[hd] + jnp.sum(p, axis=0, keepdims=True)
            probs.append(p.astype(bf16))
            alphas.append(alpha)
        for hd in range(N_HEADS):
            vv = vbt_ref[0, j, hd * HEAD_DIM:(hd + 1) * HEAD_DIM, :]
            acc_ref[hd] = alphas[hd] * acc_ref[hd] + jnp.dot(vv, probs[hd], preferred_element_type=f32)

    scores_into(sa_ref, 0)

    def body(i, carry):
        j = 2 * i
        scores_into(sb_ref, j + 1)
        consume(sa_ref, j, False)
        scores_into(sa_ref, j + 2)
        consume(sb_ref, j + 1, False)
        return carry

    lax.fori_loop(0, qi // 2, body, 0)

    @pl.when(qi % 2 == 0)
    def _():
        consume(sa_ref, qi, True)

    @pl.when(qi % 2 == 1)
    def _():
        scores_into(sb_ref, qi)
        consume(sa_ref, qi - 1, False)
        consume(sb_ref, qi, True)

    o = jnp.concatenate([acc_ref[hd] / l_ref[hd] for hd in range(N_HEADS)], axis=0)
    o_ref[0] = o.T.astype(o_ref.dtype)


def _fox_attn(qt, qtail, kt, vbt, *, n_seq, seq):
    tq = PROMPT_TM
    return pl.pallas_call(
        functools.partial(_fox_attn_kernel, tq=tq),
        grid=(n_seq, seq // tq),
        in_specs=[pl.BlockSpec((1, D_B, tq), lambda b, i: (b, 0, i)),
                  pl.BlockSpec((1, N_HEADS, TAIL_ROWS, tq), lambda b, i: (b, 0, 0, i)),
                  pl.BlockSpec((1, N_HEADS, seq, V7X_LANES), lambda b, i: (b, 0, 0, 0)),
                  pl.BlockSpec((1, seq // tq, D_B, tq), lambda b, i: (b, 0, 0, 0))],
        out_specs=pl.BlockSpec((1, tq, D_B), lambda b, i: (b, i, 0)),
        out_shape=jax.ShapeDtypeStruct((n_seq, seq, D_B), bf16),
        scratch_shapes=[pltpu.VMEM((N_HEADS, V7X_LANES, tq), bf16), pltpu.VMEM((N_HEADS, 1, tq), f32),
                        pltpu.VMEM((N_HEADS, 1, tq), f32), pltpu.VMEM((N_HEADS, HEAD_DIM, tq), f32),
                        pltpu.VMEM((N_HEADS, tq, tq), f32), pltpu.VMEM((N_HEADS, tq, tq), f32)],
        compiler_params=_params(("arbitrary", "arbitrary"), 48 << 20), name="fox_attn")(
            qt, qtail, kt, vbt)


def _page_copies(pt_ref, ck_ref, cv_ref, cl_ref, kbuf, vbuf, lbuf, sems, step, slot, *, n_chunks, n_pages, g, b0):
    b = b0 + step // n_chunks
    c = step % n_chunks
    base = b * n_pages + (n_chunks - 1 - c) * g
    copies = []
    for j in range(g):
        page = pt_ref[base + j]
        lanes = pl.ds(j * PAGE_SIZE, PAGE_SIZE)
        copies.append(pltpu.make_async_copy(ck_ref.at[page], kbuf.at[slot, :, lanes], sems.at[0, slot]))
        copies.append(pltpu.make_async_copy(cv_ref.at[page], vbuf.at[slot, :, lanes], sems.at[1, slot]))
        copies.append(pltpu.make_async_copy(cl_ref.at[page], lbuf.at[slot, :, lanes], sems.at[2, slot]))
    return copies


def _dot_f32(x, m01):
    out = None
    for piece in _split3(x):
        t = jnp.dot(piece.astype(bf16), m01, preferred_element_type=f32)
        out = t if out is None else out + t
    return out


N_SIDE_IN = 7
N_SIDE_SCRATCH = 9


def _side_begin(sr, step, *, n_steps, n_chunks, n_pages, g, n_tok, b0):
    (pt_ref, q_ref, kn_ref, vn_ref, lfn_ref, ck_ref, cv_ref, cl_ref, o_ref,
     kbuf, vbuf, lbuf, sems, m_ref, l_ref, acc_ref, run_ref, qbd_ref) = sr
    c = step % n_chunks
    slot = step % 2
    copies = functools.partial(_page_copies, pt_ref, ck_ref, cv_ref, cl_ref, kbuf, vbuf, lbuf, sems,
                               n_chunks=n_chunks, n_pages=n_pages, g=g, b0=b0)
    rows = n_tok * N_HEADS
    contract_last = (((1,), (1,)), ((), ()))

    @pl.when(step == 0)
    def _():
        for cp in copies(step, slot):
            cp.start()

    @pl.when(c == 0)
    def _():
        lane = lax.broadcasted_iota(jnp.int32, (N_HEADS, D_B), 1)
        head = lax.broadcasted_iota(jnp.int32, (N_HEADS, D_B), 0)
        own = (lane // HEAD_DIM) == head
        q4 = q_ref[0]
        qbd = jnp.concatenate([jnp.where(own, jnp.broadcast_to(q4[t:t + 1, :], (N_HEADS, D_B)), 0.0)
                               for t in range(n_tok)], axis=0)
        qbd_ref[...] = qbd.astype(bf16)
        s = lax.dot_general(qbd.astype(bf16), kn_ref[0].astype(bf16), contract_last,
                            preferred_element_type=f32)
        lfn = lfn_ref[0]
        acc_rows = [lfn[0:1, :]]
        for t in range(1, V7X_SUBLANES):
            acc_rows.append(acc_rows[-1] + lfn[t:t + 1, :])
        dq = jnp.concatenate(acc_rows, axis=0)
        dq_t = dq.T[0:N_HEADS, 0:V7X_SUBLANES]
        bias = jnp.concatenate([dq_t] * n_tok, axis=0)
        r = lax.broadcasted_iota(jnp.int32, (rows, V7X_SUBLANES), 0) // N_HEADS
        t2 = lax.broadcasted_iota(jnp.int32, (rows, V7X_SUBLANES), 1)
        s = jnp.where(t2 <= r, s - bias, NEG_BIG)
        m = jnp.max(s, axis=-1, keepdims=True)
        p = jnp.exp(s - m)
        m_ref[...] = m
        l_ref[...] = jnp.sum(p, axis=-1, keepdims=True)
        acc_ref[...] = jnp.dot(p.astype(bf16), vn_ref[0].astype(bf16), preferred_element_type=f32)
        run_ref[...] = jnp.zeros(run_ref.shape, f32)

    for cp in copies(step, slot):
        cp.wait()
    for cp in copies(jnp.where(step + 1 < n_steps, step + 1, 0), 1 - slot):
        cp.start()
    return slot


def _side_scores(sr, slot, *, g, n_tok):
    (pt_ref, q_ref, kn_ref, vn_ref, lfn_ref, ck_ref, cv_ref, cl_ref, o_ref,
     kbuf, vbuf, lbuf, sems, m_ref, l_ref, acc_ref, run_ref, qbd_ref) = sr
    s = jnp.dot(qbd_ref[...], kbuf[slot].astype(bf16), preferred_element_type=f32)

    lfc = lbuf[slot]
    lfp = jnp.concatenate([lfc[:, j * PAGE_SIZE:(j + 1) * PAGE_SIZE] for j in range(g)], axis=0)
    jj = lax.broadcasted_iota(jnp.int32, (PAGE_SIZE, PAGE_SIZE), 0)
    ss = lax.broadcasted_iota(jnp.int32, (PAGE_SIZE, PAGE_SIZE), 1)
    later = jnp.where(jj > ss, 1.0, 0.0).astype(bf16)
    within = _dot_f32(lfp, later)
    total = jnp.sum(lfp, axis=-1, keepdims=True)
    run = run_ref[:, 0:1]
    tiles = [None] * g
    for j in reversed(range(g)):
        tiles[j] = within[j * N_HEADS:(j + 1) * N_HEADS, :] + run
        run = run + total[j * N_HEADS:(j + 1) * N_HEADS, :]
    run_ref[...] = jnp.broadcast_to(run, run_ref.shape)
    bias = jnp.concatenate(tiles, axis=1)
    return s + jnp.concatenate([bias] * n_tok, axis=0)


def _side_update(sr, s, slot):
    (pt_ref, q_ref, kn_ref, vn_ref, lfn_ref, ck_ref, cv_ref, cl_ref, o_ref,
     kbuf, vbuf, lbuf, sems, m_ref, l_ref, acc_ref, run_ref, qbd_ref) = sr
    contract_last = (((1,), (1,)), ((), ()))
    m_old = m_ref[...]
    m_new = jnp.maximum(m_old, jnp.max(s, axis=-1, keepdims=True))
    alpha = jnp.exp(m_old - m_new)
    p = jnp.exp(s - m_new)
    m_ref[...] = m_new
    l_ref[...] = alpha * l_ref[...] + jnp.sum(p, axis=-1, keepdims=True)
    pv = lax.dot_general(p.astype(bf16), vbuf[slot].astype(bf16), contract_last, preferred_element_type=f32)
    acc_ref[...] = alpha * acc_ref[...] + pv


def _side_end(sr, step, slot, *, n_steps, n_chunks, n_pages, g, n_tok, b0):
    (pt_ref, q_ref, kn_ref, vn_ref, lfn_ref, ck_ref, cv_ref, cl_ref, o_ref,
     kbuf, vbuf, lbuf, sems, m_ref, l_ref, acc_ref, run_ref, qbd_ref) = sr

    @pl.when(step == n_steps - 1)
    def _():
        for cp in _page_copies(pt_ref, ck_ref, cv_ref, cl_ref, kbuf, vbuf, lbuf, sems, 0, 1 - slot,
                               n_chunks=n_chunks, n_pages=n_pages, g=g, b0=b0):
            cp.wait()

    @pl.when(step % n_chunks == n_chunks - 1)
    def _():
        lane = lax.broadcasted_iota(jnp.int32, (N_HEADS, D_B), 1)
        head = lax.broadcasted_iota(jnp.int32, (N_HEADS, D_B), 0)
        own = (lane // HEAD_DIM) == head
        o = acc_ref[...] / l_ref[...]
        outs = [jnp.sum(jnp.where(own, o[t * N_HEADS:(t + 1) * N_HEADS, :], 0.0), axis=0, keepdims=True)
                for t in range(n_tok)]
        outs += [jnp.zeros((V7X_SUBLANES - n_tok, D_B), f32)]
        o_ref[0] = jnp.concatenate(outs, axis=0)


def _host_kernel(*refs, body, n_in, n_out, n_tiles, side, **static):
    if side is None:
        body(*refs, n_tiles=n_tiles, midway=lambda: None, **static)
        return
    pt_ref, rest = refs[0], refs[1:]
    host_in, side_in, rest = rest[:n_in], rest[n_in:n_in + N_SIDE_IN], rest[n_in + N_SIDE_IN:]
    host_out, side_out, rest = rest[:n_out], rest[n_out], rest[n_out + 1:]
    host_scratch, side_scratch = rest[:-N_SIDE_SCRATCH], rest[-N_SIDE_SCRATCH:]
    sr = (pt_ref, *side_in, side_out, *side_scratch)
    n_steps = pl.num_programs(0) * n_tiles
    step = pl.program_id(0) * n_tiles + pl.program_id(1)
    slot = _side_begin(sr, step, n_steps=n_steps, **side)
    s = _side_scores(sr, slot, g=side["g"], n_tok=side["n_tok"])
    body(*host_in, *host_out, *host_scratch, n_tiles=n_tiles, midway=functools.partial(_side_update, sr, s, slot),
         **static)
    _side_end(sr, step, slot, n_steps=n_steps, **side)


def _host_call(body, name, *, grid, in_specs, args, out_specs, out_shape, scratch, vmem, static, side=None):
    n_tiles = grid[1]
    kernel = functools.partial(_host_kernel, body=body, n_in=len(in_specs), n_out=len(out_specs), n_tiles=n_tiles,
                               **static)
    if side is None:
        return pl.pallas_call(functools.partial(kernel, side=None), grid=grid, in_specs=in_specs,
                              out_specs=out_specs, out_shape=out_shape, scratch_shapes=scratch,
                              compiler_params=_params(("arbitrary", "arbitrary"), vmem), name=name)(*args)
    b0, page_table, n_tok, side_args = side
    n_pages = page_table.shape[1]
    g = PAGES_PER_STEP
    n_chunks = n_pages // g
    n_side = grid[0] * n_tiles // n_chunks
    assert n_side * n_chunks == grid[0] * n_tiles
    rows = n_tok * N_HEADS
    seq_of = lambda s, t: (s * n_tiles + t) // n_chunks
    tok = lambda c: pl.BlockSpec((1, V7X_SUBLANES, c), lambda s, t, *_: (b0 + seq_of(s, t), 0, 0))
    anyspec = pl.BlockSpec(memory_space=pl.ANY)
    grid_spec = pltpu.PrefetchScalarGridSpec(
        num_scalar_prefetch=1, grid=grid,
        in_specs=list(in_specs) + [tok(D_B), tok(D_B), tok(D_B), tok(V7X_LANES), anyspec, anyspec, anyspec],
        out_specs=list(out_specs) + [pl.BlockSpec((1, V7X_SUBLANES, D_B), lambda s, t, *_: (seq_of(s, t), 0, 0))],
        scratch_shapes=list(scratch) + [
            pltpu.VMEM((2, D_B, g * PAGE_SIZE), f32), pltpu.VMEM((2, D_B, g * PAGE_SIZE), f32),
            pltpu.VMEM((2, N_HEADS, g * PAGE_SIZE), f32), pltpu.SemaphoreType.DMA((3, 2)),
            pltpu.VMEM((rows, 1), f32), pltpu.VMEM((rows, 1), f32), pltpu.VMEM((rows, D_B), f32),
            pltpu.VMEM((N_HEADS, V7X_LANES), f32), pltpu.VMEM((rows, D_B), bf16)])
    side_vmem = 4 * g * PAGE_SIZE * D_B * 4 + 6 * g * PAGE_SIZE * D_B * 2
    cfg = dict(n_chunks=n_chunks, n_pages=n_pages, g=g, n_tok=n_tok, b0=b0)
    return pl.pallas_call(
        functools.partial(kernel, side=cfg), grid_spec=grid_spec,
        out_shape=list(out_shape) + [jax.ShapeDtypeStruct((n_side, V7X_SUBLANES, D_B), f32)],
        compiler_params=_params(("arbitrary", "arbitrary"), vmem + side_vmem), name=name)(
            page_table.reshape(-1), *args, *side_args)


def _out_proj_kernel(x_ref, ya_ref, yb_ref, wa_ref, wb_ref, o_ref):
    y = jnp.dot(ya_ref[...], wa_ref[...], preferred_element_type=f32)
    y = y + jnp.dot(yb_ref[...], wb_ref[...], preferred_element_type=f32)
    o_ref[...] = x_ref[...] + y


def _out_proj(x, ya, yb, wa, wb, *, tm):
    n = x.shape[0]
    row = lambda c: pl.BlockSpec((tm, c), lambda i: (i, 0))
    return pl.pallas_call(
        _out_proj_kernel, grid=(n // tm,),
        in_specs=[row(D_MODEL), row(D_A), row(D_B), _const_spec(wa.shape), _const_spec(wb.shape)],
        out_specs=row(D_MODEL), out_shape=jax.ShapeDtypeStruct((n, D_MODEL), f32),
        compiler_params=_params(("arbitrary",), 32 << 20), name="out_proj")(x, ya, yb, wa, wb)


def _ffn_kernel(*refs, tm, stride, n_tiles, has_hist, midway):
    if has_hist:
        x_ref, g_ref, wu_ref, cw_ref, wd_ref, hist_ref, o_ref, st_ref, ext_ref, a_ref = refs
    else:
        x_ref, g_ref, wu_ref, cw_ref, wd_ref, o_ref, st_ref, ext_ref, a_ref = refs
        hist_ref = None
    xn = _rmsnorm(x_ref[...], g_ref[...]).astype(bf16)
    _conv_init(ext_ref, hist_ref, width=CONV_FFN, stride=stride, tile=pl.program_id(1))
    for c0 in range(0, D_FF, FFN_CHUNK):
        if c0 == D_FF // FFN_CHUNK // 2 * FFN_CHUNK:
            midway()
        cols = slice(c0, c0 + FFN_CHUNK)
        gate = jnp.dot(xn, wu_ref[:, c0:c0 + FFN_CHUNK], preferred_element_type=f32)
        val = jnp.dot(xn, wu_ref[:, D_FF + c0:D_FF + c0 + FFN_CHUNK], preferred_element_type=f32)
        gconv = _causal_conv(ext_ref, gate, cw_ref, st_ref, width=CONV_FFN, stride=stride, tm=tm,
                             n_tiles=n_tiles, cols=cols)
        a_ref[:, cols] = (gconv * _sigmoid(gconv) * val).astype(bf16)
    o_ref[...] = x_ref[...] + jnp.dot(a_ref[...], wd_ref[...], preferred_element_type=f32)


def _ffn(x, g, wu, cw, wd, hist, *, layer, n_seq, n_tiles, tm, stride, side=None):
    n = x.shape[0]
    hp, h = _conv_rows(CONV_FFN, stride)
    row = pl.BlockSpec((tm, D_MODEL), lambda s, t, *_: (s * n_tiles + t, 0))
    slab = lambda a: pl.BlockSpec((None,) + a.shape[1:], lambda *_: (layer, 0, 0), pipeline_mode=pl.Buffered(1))
    in_specs = [row, _const_spec(g.shape), slab(wu), _const_spec(cw.shape), slab(wd)]
    args = [x, g, wu, cw, wd]
    if hist is not None:
        in_specs.append(_const_spec(hist.shape))
        args.append(hist)
    weights = 2 * (wu.size + wd.size) // wu.shape[0]
    vmem = weights + 16 * tm * D_MODEL + (hp + tm) * D_FF * 4 + tm * D_FF * 2 + (12 << 20)
    return _host_call(
        _ffn_kernel, "conv_ffn", grid=(n_seq, n_tiles), in_specs=in_specs, args=args,
        out_specs=[row, pl.BlockSpec((1, h, D_FF), lambda s, t, *_: (s, 0, 0))],
        out_shape=[jax.ShapeDtypeStruct((n, D_MODEL), f32), jax.ShapeDtypeStruct((n_seq, h, D_FF), f32)],
        scratch=[pltpu.VMEM((hp + tm, D_FF), f32), pltpu.VMEM((tm, D_FF), bf16)],
        vmem=vmem, static=dict(tm=tm, stride=stride, has_hist=hist is not None), side=side)


def _conformer_kernel(*refs, tm, stride, n_tiles, has_hist, grouped, midway):
    refs = list(refs)
    group_ref = refs.pop() if grouped else None
    if has_hist:
        (x_ref, g_ref, w1_ref, b1_ref, cw_ref, cb_ref, lg_ref, lb_ref, w2_ref, hist_ref,
         o_ref, st_ref, ext_ref) = refs
    else:
        (x_ref, g_ref, w1_ref, b1_ref, cw_ref, cb_ref, lg_ref, lb_ref, w2_ref,
         o_ref, st_ref, ext_ref) = refs
        hist_ref = None
    xn = _rmsnorm(x_ref[...], g_ref[...]).astype(bf16)
    _conv_init(ext_ref, hist_ref, width=CONV_C, stride=stride, tile=pl.program_id(1))
    parts = []
    for j, c0 in enumerate(range(0, D_C, CONV_CHUNK)):
        if c0 == D_C // 2:
            midway()
        cols = slice(c0, c0 + CONV_CHUNK)
        gcols = slice(D_C + c0, D_C + c0 + CONV_CHUNK)
        a = jnp.dot(xn, w1_ref[:, cols], preferred_element_type=f32) + b1_ref[:, cols]
        g = jnp.dot(xn, w1_ref[:, gcols], preferred_element_type=f32) + b1_ref[:, gcols]
        y = _causal_conv(ext_ref, a * _sigmoid(g), cw_ref, st_ref, width=CONV_C, stride=stride, tm=tm,
                         n_tiles=n_tiles, cols=cols, group_ref=group_ref.at[j] if grouped else None)
        parts.append(y + cb_ref[:, cols])
    y = jnp.concatenate(parts, axis=1)
    mu = jnp.mean(y, axis=-1, keepdims=True)
    yc = y - mu
    yn = yc * lax.rsqrt(jnp.mean(yc * yc, axis=-1, keepdims=True) + EPS) * lg_ref[...] + lb_ref[...]
    a = yn * _sigmoid(yn)
    o_ref[...] = x_ref[...] + jnp.dot(a.astype(bf16), w2_ref[...], preferred_element_type=f32)


def _conformer(x, g, w1, b1, cw, cb, lg, lb, w2, hist, *, n_seq, n_tiles, tm, stride, side=None):
    n = x.shape[0]
    hp, h = _conv_rows(CONV_C, stride)
    grouped = stride % V7X_SUBLANES != 0
    row = pl.BlockSpec((tm, D_MODEL), lambda s, t, *_: (s * n_tiles + t, 0))
    consts = [g, w1, b1, cw, cb, lg, lb, w2] + ([hist] if hist is not None else [])
    scratch = [pltpu.VMEM((hp + tm, D_C), f32)]
    group_bytes = 0
    if grouped:
        group_shape = (D_C // CONV_CHUNK, V7X_SUBLANES - 1, tm + V7X_SUBLANES, CONV_CHUNK)
        scratch.append(pltpu.VMEM(group_shape, f32))
        group_bytes = 4 * group_shape[0] * group_shape[1] * group_shape[2] * group_shape[3]
    vmem = 2 * (w1.size + w2.size) + 16 * tm * D_MODEL + 6 * (hp + tm) * D_C * 4 + group_bytes + (8 << 20)
    return _host_call(
        _conformer_kernel, "conformer", grid=(n_seq, n_tiles),
        in_specs=[row] + [_const_spec(a.shape) for a in consts], args=[x] + consts,
        out_specs=[row, pl.BlockSpec((1, h, D_C), lambda s, t, *_: (s, 0, 0))],
        out_shape=[jax.ShapeDtypeStruct((n, D_MODEL), f32), jax.ShapeDtypeStruct((n_seq, h, D_C), f32)],
        scratch=scratch, vmem=vmem,
        static=dict(tm=tm, stride=stride, has_hist=hist is not None, grouped=grouped), side=side)


def _time_major(a):
    return jnp.swapaxes(a, 0, 1).reshape(-1, a.shape[-1])


def _batch_major(a, n_batch):
    return jnp.swapaxes(a.reshape(-1, n_batch, a.shape[-1]), 0, 1)


def _pad_tokens(a):
    return jnp.pad(a, ((0, 0), (0, V7X_SUBLANES - a.shape[1]), (0, 0)))


def _heads_last(a_t, n_seq, seq):
    return jnp.transpose(a_t.reshape(n_seq, N_HEADS, HEAD_DIM, seq), (0, 3, 1, 2))


def kernel(x_prompt, x_sample, cache_k, cache_v, cache_logf, page_table, state_conv_a, state_conv_c, state_ffn, norm_mix0, w_in0, b_forget, conv_a_w, q_norm, k_norm, w_out0, norm_mix1, w_pw1, b_pw1, conv_c_w, conv_c_b, ln_c_g, ln_c_b, w_pw2, norm_ffn, w_up, conv_ffn_w, w_down):
    n_batch, seq, _ = x_prompt.shape
    dec_batch, dec_seq, _ = x_sample.shape
    n_pool = cache_k.shape[1]
    p_tiles = seq // PROMPT_TM
    prompt = dict(n_seq=n_batch, n_tiles=p_tiles, tm=PROMPT_TM, stride=1)
    sample = dict(n_seq=1, n_tiles=1, tm=dec_batch * dec_seq, stride=dec_batch)
    vec = lambda a: a.reshape(1, -1)

    yp = x_prompt.reshape(n_batch * seq, D_MODEL)
    ys = _time_major(x_sample)

    w_in = jnp.pad(w_in0[0], ((0, 0), (0, D_IN_PAD - w_in0.shape[2]))).astype(bf16)
    b_f = jnp.pad(b_forget[0], (0, V7X_LANES - N_HEADS)).reshape(1, V7X_LANES)
    q_g = vec(jnp.tile(q_norm[0], N_HEADS))
    k_g = vec(jnp.tile(k_norm[0], N_HEADS))
    lane_head = jnp.arange(D_B) // HEAD_DIM
    head_mean = jnp.where(lane_head[:, None] == lane_head[None, :], 1.0 / HEAD_DIM, 0.0).astype(bf16)
    w_oa = w_out0[0, :D_A].astype(bf16)
    w_ob = w_out0[0, D_A:].astype(bf16)
    proj = (vec(norm_mix0[0]), w_in, b_f, conv_a_w[0], q_g, k_g, head_mean)

    ya_s, q_ts, k_s, _, v_ts, _, lf_s, _, sca = _in_proj(ys, *proj, _time_major(state_conv_a[0]), **sample)
    to_batch = lambda a: _batch_major(a, dec_batch)
    q_s = to_batch(q_ts[0].T)
    v_s = to_batch(v_ts[0].T)
    side_args = (_pad_tokens(q_s), _pad_tokens(to_batch(k_s)), _pad_tokens(v_s), _pad_tokens(to_batch(lf_s)),
                 jnp.transpose(cache_k[0], (0, 2, 3, 1)).reshape(n_pool, D_B, PAGE_SIZE),
                 jnp.transpose(cache_v[0], (0, 2, 3, 1)).reshape(n_pool, D_B, PAGE_SIZE),
                 jnp.swapaxes(cache_logf[0], 1, 2))
    per_host = n_batch * p_tiles // (page_table.shape[1] // PAGES_PER_STEP)
    assert 4 * per_host == dec_batch
    side = lambda i: (i * per_host, page_table, dec_seq, side_args)
    yb_parts = []

    ya, q_t, k, k_t, v_t, vb_t, lf, lf_t, pca, part = _in_proj(yp, *proj, None, **prompt, side=side(0))
    yb_parts.append(part)
    kt, qtail = _fox_prep(lf, k, n_seq=n_batch, seq=seq)
    yb = _fox_attn(q_t, qtail, kt, vb_t, n_seq=n_batch, seq=seq)
    yp = _out_proj(yp, ya, yb.reshape(n_batch * seq, D_B), w_oa, w_ob, tm=WIDE_TM)
    p_k = _heads_last(k_t, n_batch, seq)[None]
    p_v = _heads_last(v_t, n_batch, seq)[None]
    p_logf = jnp.swapaxes(lf_t, 1, 2)[None]

    wu = w_up.astype(bf16)
    wd = w_down.astype(bf16)
    conf = (vec(norm_mix1[0]), w_pw1[0].astype(bf16), vec(b_pw1[0]), conv_c_w[0], vec(conv_c_b[0]),
            vec(ln_c_g[0]), vec(ln_c_b[0]), w_pw2[0].astype(bf16))
    p_ffn = []
    for layer in range(2):
        if layer == 1:
            yp, pcc, part = _conformer(yp, *conf, None, **prompt, side=side(2))
            yb_parts.append(part)
        yp, pst, part = _ffn(yp, vec(norm_ffn[layer]), wu, conv_ffn_w[layer], wd, None, layer=layer, **prompt,
                             side=side(1 + 2 * layer))
        yb_parts.append(part)
        p_ffn.append(pst)

    yb_s = _time_major(jnp.concatenate(yb_parts, axis=0)[:, :dec_seq]).astype(bf16)
    ys = _out_proj(ys, ya_s, yb_s, w_oa, w_ob, tm=dec_batch * dec_seq)
    s_k = to_batch(k_s).reshape(1, dec_batch, dec_seq, N_HEADS, HEAD_DIM)
    s_v = v_s.reshape(1, dec_batch, dec_seq, N_HEADS, HEAD_DIM)
    s_logf = to_batch(lf_s)[:, :, :N_HEADS].reshape(1, dec_batch, dec_seq, N_HEADS)
    s_conv_a = to_batch(sca[0])[None]
    s_ffn = []
    for layer in range(2):
        if layer == 1:
            ys, scc = _conformer(ys, *conf, _time_major(state_conv_c[0]), **sample)
        ys, sst = _ffn(ys, vec(norm_ffn[layer]), wu, conv_ffn_w[layer], wd, _time_major(state_ffn[layer]),
                       layer=layer, **sample)
        s_ffn.append(to_batch(sst[0]))

    return (yp.reshape(n_batch, seq, D_MODEL), to_batch(ys),
            pca[None], s_conv_a,
            p_k, s_k, p_v, s_v, p_logf, s_logf,
            pcc[None], to_batch(scc[0])[None],
            jnp.stack(p_ffn), jnp.stack(s_ffn))
```

```python
import functools

import jax
import jax.numpy as jnp
from jax import lax
from jax.experimental import pallas as pl
from jax.experimental.pallas import tpu as pltpu

D_MODEL = 1024
HEAD_DIM = 64
N_HEADS = 8
D_A = 512
D_B = 512
D_C = 1024
D_FF = 2816
CONV_A = 3
CONV_C = 31
CONV_FFN = 3
PAGE_SIZE = 128
EPS = 1e-6

V7X_LANES = 128
V7X_SUBLANES = 8
V7X_VMEM_BYTES = 64 * 1024 * 1024
NEG_BIG = -1e30
LOG2E = 1.4426950408889634

D_IN_PAD = 3 * D_A + 3 * D_B + V7X_LANES
BIAS_LANE = HEAD_DIM
TAIL_ROWS = V7X_SUBLANES
VT_ROWS = HEAD_DIM + 16

PROMPT_TM = 256
WIDE_TM = 512
FFN_CHUNK = 256
CONV_CHUNK = 256
PREP_TS = 1024
PAGES_PER_STEP = 16

bf16 = jnp.bfloat16
f32 = jnp.float32


def _round_up(n, m):
    return (n + m - 1) // m * m


def _const_spec(shape):
    zeros = (0,) * len(shape)
    return pl.BlockSpec(shape, lambda *_: zeros, pipeline_mode=pl.Buffered(1))


def _params(semantics, vmem_bytes):
    return pltpu.CompilerParams(dimension_semantics=semantics,
                                vmem_limit_bytes=min(int(vmem_bytes), V7X_VMEM_BYTES - (4 << 20)))


def _rmsnorm(x, g):
    return x * lax.rsqrt(jnp.mean(x * x, axis=-1, keepdims=True) + EPS) * g


def _sigmoid(x):
    return 1.0 / (1.0 + jnp.exp(-x))


def _conv_rows(width, stride):
    h = (width - 1) * stride
    return _round_up(h, V7X_SUBLANES), h


def _conv_init(ext_ref, hist_ref, *, width, stride, tile):
    hp, h = _conv_rows(width, stride)

    @pl.when(tile == 0)
    def _():
        if hist_ref is None:
            ext_ref[0:hp, :] = jnp.zeros((hp, ext_ref.shape[1]), f32)
        else:
            ext_ref[hp - h:hp, :] = hist_ref[...]


def _causal_conv(ext_ref, u, w_ref, state_ref, *, width, stride, tm, n_tiles, cols=slice(None), group_ref=None):
    hp, h = _conv_rows(width, stride)
    ext_ref[hp:hp + tm, cols] = u
    starts = [hp - (width - 1 - k) * stride for k in range(width)]
    y = None
    if group_ref is None:
        for k, start in enumerate(starts):
            term = w_ref[k:k + 1, cols] * ext_ref[start:start + tm, cols]
            y = term if y is None else y + term
    else:
        for off in range(V7X_SUBLANES):
            taps = [k for k, start in enumerate(starts) if start % V7X_SUBLANES == off]
            if not taps:
                continue
            rows = tm if off == 0 else tm + V7X_SUBLANES
            part = None
            for k in taps:
                base = starts[k] - off
                term = w_ref[k:k + 1, cols] * ext_ref[base:base + rows, cols]
                part = term if part is None else part + term
            if off:
                group_ref[off - 1, 0:rows, :] = part
                part = group_ref[off - 1, off:off + tm, :]
            y = part if y is None else y + part
    state_ref[0, :, cols] = ext_ref[hp + tm - h:hp + tm, cols]
    if n_tiles > 1:
        ext_ref[0:hp, cols] = ext_ref[tm:tm + hp, cols]
    return y


def _in_proj_kernel(*refs, tm, stride, n_tiles, has_hist, midway):
    if has_hist:
        (x_ref, g_ref, w_ref, bf_ref, cw_ref, qg_ref, kg_ref, bd_ref, hist_ref,
         ya_ref, qt_ref, k_ref, kt_ref, vt_ref, vbt_ref, lf_ref, lft_ref, st_ref, ext_ref) = refs
    else:
        (x_ref, g_ref, w_ref, bf_ref, cw_ref, qg_ref, kg_ref, bd_ref,
         ya_ref, qt_ref, k_ref, kt_ref, vt_ref, vbt_ref, lf_ref, lft_ref, st_ref, ext_ref) = refs
        hist_ref = None
    tile = pl.program_id(1)
    xn = _rmsnorm(x_ref[...], g_ref[...]).astype(bf16)
    _conv_init(ext_ref, hist_ref, width=CONV_A, stride=stride, tile=tile)
    h = jnp.dot(xn, w_ref[:, 0:3 * D_A], preferred_element_type=f32)
    u = h[:, D_A:2 * D_A] * h[:, 2 * D_A:3 * D_A]
    yc = _causal_conv(ext_ref, u, cw_ref, st_ref, width=CONV_A, stride=stride, tm=tm, n_tiles=n_tiles)
    ya_ref[...] = (h[:, 0:D_A] * yc).astype(bf16)
    midway()
    h = jnp.dot(xn, w_ref[:, 3 * D_A:], preferred_element_type=f32)
    q = h[:, 0:D_B]
    k = h[:, D_B:2 * D_B]
    v = h[:, 2 * D_B:3 * D_B]
    fl = h[:, 3 * D_B:]

    def head_norm(z, gain):
        ms = jnp.dot((z * z).astype(bf16), bd_ref[...], preferred_element_type=f32)
        return z * lax.rsqrt(ms + EPS) * gain

    qt_ref[0] = (head_norm(q, qg_ref[...]) * (HEAD_DIM ** -0.5)).T
    kn = head_norm(k, kg_ref[...])
    k_ref[...] = kn
    kt_ref[0] = kn.T
    vt = v.T
    vt_ref[0] = vt
    ones = jnp.ones((VT_ROWS - HEAD_DIM, vt.shape[1]), f32)
    vbt_ref[0, 0] = jnp.concatenate(
        [blk for hd in range(N_HEADS) for blk in (vt[hd * HEAD_DIM:(hd + 1) * HEAD_DIM, :], ones)], axis=0).astype(bf16)
    z = fl + bf_ref[...]
    logf = jnp.minimum(z, 0.0) - jnp.log(1.0 + jnp.exp(-jnp.abs(z)))
    lane = lax.broadcasted_iota(jnp.int32, logf.shape, 1)
    logf = jnp.where(lane < N_HEADS, logf, 0.0)
    lf_ref[...] = logf
    lft_ref[0] = logf.T[0:N_HEADS, :]


def _in_proj(x, g, w, bfv, cw, qg, kg, bd, hist, *, n_seq, n_tiles, tm, stride, side=None):
    n = x.shape[0]
    seq = n_tiles * tm
    hp, h = _conv_rows(CONV_A, stride)
    row = lambda c: pl.BlockSpec((tm, c), lambda s, t, *_: (s * n_tiles + t, 0))
    col = lambda r: pl.BlockSpec((1, r, tm), lambda s, t, *_: (s, 0, t))
    in_specs = [row(D_MODEL), _const_spec(g.shape), _const_spec(w.shape), _const_spec(bfv.shape),
                _const_spec(cw.shape), _const_spec(qg.shape), _const_spec(kg.shape), _const_spec(bd.shape)]
    args = [x, g, w, bfv, cw, qg, kg, bd]
    if hist is not None:
        in_specs.append(_const_spec(hist.shape))
        args.append(hist)
    out_shape = [jax.ShapeDtypeStruct((n, D_A), bf16),
                 jax.ShapeDtypeStruct((n_seq, D_B, seq), f32),
                 jax.ShapeDtypeStruct((n, D_B), f32),
                 jax.ShapeDtypeStruct((n_seq, D_B, seq), f32),
                 jax.ShapeDtypeStruct((n_seq, D_B, seq), f32),
                 jax.ShapeDtypeStruct((n_seq, n_tiles, N_HEADS * VT_ROWS, tm), bf16),
                 jax.ShapeDtypeStruct((n, V7X_LANES), f32),
                 jax.ShapeDtypeStruct((n_seq, N_HEADS, seq), f32),
                 jax.ShapeDtypeStruct((n_seq, h, D_A), f32)]
    out_specs = [row(D_A), col(D_B), row(D_B), col(D_B), col(D_B),
                 pl.BlockSpec((1, 1, N_HEADS * VT_ROWS, tm), lambda s, t, *_: (s, t, 0, 0)),
                 row(V7X_LANES), col(N_HEADS),
                 pl.BlockSpec((1, h, D_A), lambda s, t, *_: (s, 0, 0))]
    vmem = 2 * w.size + 10 * tm * D_IN_PAD * 4 + (8 << 20)
    return _host_call(_in_proj_kernel, "in_proj", grid=(n_seq, n_tiles), in_specs=in_specs, args=args,
                      out_specs=out_specs, out_shape=out_shape, scratch=[pltpu.VMEM((hp + tm, D_A), f32)],
                      vmem=vmem, static=dict(tm=tm, stride=stride, has_hist=hist is not None), side=side)


def _split3(x):
    hi = x.astype(bf16).astype(f32)
    r = x - hi
    mid = r.astype(bf16).astype(f32)
    return hi, mid, r - mid


def _fox_prep_kernel(lf_ref, k_ref, kt_ref, qtail_ref, carry_ref, *, ts):
    @pl.when(pl.program_id(1) == 0)
    def _():
        carry_ref[...] = jnp.zeros(carry_ref.shape, f32)

    c = lf_ref[...]
    row = lax.broadcasted_iota(jnp.int32, c.shape, 0)
    shift = 1
    while shift < ts:
        c = c + jnp.where(row >= shift, pltpu.roll(c, shift, axis=0), 0.0)
        shift *= 2
    c = c + carry_ref[0:1, :]
    carry_ref[...] = jnp.broadcast_to(c[ts - 1:ts, :], carry_ref.shape)
    c = c * LOG2E
    c_t = c.T

    lane = lax.broadcasted_iota(jnp.int32, (ts, V7X_LANES), 1)
    ones = jnp.where((lane >= BIAS_LANE) & (lane < BIAS_LANE + 3), 1.0, 0.0)
    one_row = jnp.ones((1, ts), f32)
    zero_row = jnp.zeros((1, ts), f32)
    for hd in range(N_HEADS):
        hi, mid, lo = _split3(jnp.broadcast_to(c[:, hd:hd + 1], (ts, V7X_LANES)))
        neg = jnp.where(lane == BIAS_LANE + 3, -hi, jnp.where(lane == BIAS_LANE + 4, -mid,
                        jnp.where(lane == BIAS_LANE + 5, -lo, ones)))
        g0 = (hd // 2) * V7X_LANES
        ks = k_ref[:, g0:g0 + V7X_LANES]
        if hd % 2:
            ks = pltpu.roll(ks, HEAD_DIM, axis=1)
        kt_ref[0, hd] = jnp.where(lane < HEAD_DIM, ks, neg).astype(bf16)
        hi_t, mid_t, lo_t = _split3(c_t[hd:hd + 1, :])
        qtail_ref[0, hd] = jnp.concatenate([hi_t, mid_t, lo_t, one_row, one_row, one_row, zero_row, zero_row],
                                           axis=0)


def _fox_prep(lf, k, *, n_seq, seq):
    ts = PREP_TS
    n_tiles = seq // ts
    row = lambda c: pl.BlockSpec((ts, c), lambda s, t, *_: (s * n_tiles + t, 0))
    return pl.pallas_call(
        functools.partial(_fox_prep_kernel, ts=ts),
        grid=(n_seq, n_tiles), in_specs=[row(V7X_LANES), row(D_B)],
        out_specs=[pl.BlockSpec((1, N_HEADS, ts, V7X_LANES), lambda s, t: (s, 0, t, 0)),
                   pl.BlockSpec((1, N_HEADS, TAIL_ROWS, ts), lambda s, t: (s, 0, 0, t))],
        out_shape=[jax.ShapeDtypeStruct((n_seq, N_HEADS, seq, V7X_LANES), bf16),
                   jax.ShapeDtypeStruct((n_seq, N_HEADS, TAIL_ROWS, seq), f32)],
        scratch_shapes=[pltpu.VMEM((V7X_SUBLANES, V7X_LANES), f32)],
        compiler_params=_params(("arbitrary", "arbitrary"), 40 << 20), name="fox_prep")(lf, k)


def _fox_attn_kernel(qt_ref, qtail_ref, kt_ref, vbt_ref, o_ref, qa_ref, m_ref, acc_ref, sa_ref, sb_ref, *, tq):
    qi = pl.program_id(1)
    pad = jnp.zeros((V7X_LANES - HEAD_DIM - TAIL_ROWS, tq), f32)
    for hd in range(N_HEADS):
        qa = jnp.concatenate([qt_ref[0, hd * HEAD_DIM:(hd + 1) * HEAD_DIM, :] * LOG2E, qtail_ref[0, hd], pad], axis=0)
        qa_ref[hd] = qa.astype(bf16)
    m_ref[...] = jnp.full(m_ref.shape, NEG_BIG, f32)
    acc_ref[...] = jnp.zeros(acc_ref.shape, f32)

    def scores_into(s_ref, j):
        ks = pl.multiple_of(j * tq, tq)
        for hd in range(N_HEADS):
            s_ref[hd] = jnp.dot(kt_ref[0, hd, pl.ds(ks, tq), :], qa_ref[hd],
                                preferred_element_type=f32)

    def consume(s_ref, j, diagonal):
        probs, alphas = [], []
        for hd in range(N_HEADS):
            s = s_ref[hd]
            if diagonal:
                key = lax.broadcasted_iota(jnp.int32, s.shape, 0)
                qry = lax.broadcasted_iota(jnp.int32, s.shape, 1)
                s = jnp.where(key <= qry, s, NEG_BIG)
            m_old = m_ref[hd]
            m_new = jnp.maximum(m_old, jnp.max(s, axis=0, keepdims=True))
            alpha = jnp.exp2(m_old - m_new)
            p = jnp.exp2(s - m_new)
            m_ref[hd] = m_new
            probs.append(p.astype(bf16))
            alphas.append(alpha)
        for hd in range(N_HEADS):
            vv = vbt_ref[0, j, hd * VT_ROWS:(hd + 1) * VT_ROWS, :]
            acc_ref[hd] = alphas[hd] * acc_ref[hd] + jnp.dot(vv, probs[hd], preferred_element_type=f32)

    scores_into(sa_ref, 0)

    def body(i, carry):
        j = 2 * i
        scores_into(sb_ref, j + 1)
        consume(sa_ref, j, False)
        scores_into(sa_ref, j + 2)
        consume(sb_ref, j + 1, False)
        return carry

    lax.fori_loop(0, qi // 2, body, 0)

    @pl.when(qi % 2 == 0)
    def _():
        consume(sa_ref, qi, True)

    @pl.when(qi % 2 == 1)
    def _():
        scores_into(sb_ref, qi)
        consume(sa_ref, qi - 1, False)
        consume(sb_ref, qi, True)

    o = jnp.concatenate([acc_ref[hd, 0:HEAD_DIM, :] / acc_ref[hd, HEAD_DIM:HEAD_DIM + 1, :]
                         for hd in range(N_HEADS)], axis=0)
    o_ref[0] = o.T.astype(o_ref.dtype)


def _fox_attn(qt, qtail, kt, vbt, *, n_seq, seq):
    tq = PROMPT_TM
    return pl.pallas_call(
        functools.partial(_fox_attn_kernel, tq=tq),
        grid=(n_seq, seq // tq),
        in_specs=[pl.BlockSpec((1, D_B, tq), lambda b, i: (b, 0, i)),
                  pl.BlockSpec((1, N_HEADS, TAIL_ROWS, tq), lambda b, i: (b, 0, 0, i)),
                  pl.BlockSpec((1, N_HEADS, seq, V7X_LANES), lambda b, i: (b, 0, 0, 0)),
                  pl.BlockSpec((1, seq // tq, N_HEADS * VT_ROWS, tq), lambda b, i: (b, 0, 0, 0))],
        out_specs=pl.BlockSpec((1, tq, D_B), lambda b, i: (b, i, 0)),
        out_shape=jax.ShapeDtypeStruct((n_seq, seq, D_B), bf16),
        scratch_shapes=[pltpu.VMEM((N_HEADS, V7X_LANES, tq), bf16), pltpu.VMEM((N_HEADS, 1, tq), f32),
                        pltpu.VMEM((N_HEADS, VT_ROWS, tq), f32),
                        pltpu.VMEM((N_HEADS, tq, tq), f32), pltpu.VMEM((N_HEADS, tq, tq), f32)],
        compiler_params=_params(("arbitrary", "arbitrary"), 48 << 20), name="fox_attn")(
            qt, qtail, kt, vbt)


def _page_copies(pt_ref, ck_ref, cv_ref, cl_ref, kbuf, vbuf, lbuf, sems, step, slot, *, n_chunks, n_pages, g, b0):
    b = b0 + step // n_chunks
    c = step % n_chunks
    base = b * n_pages + (n_chunks - 1 - c) * g
    copies = []
    for j in range(g):
        page = pt_ref[base + j]
        lanes = pl.ds(j * PAGE_SIZE, PAGE_SIZE)
        copies.append(pltpu.make_async_copy(ck_ref.at[page], kbuf.at[slot, :, lanes], sems.at[0, slot]))
        copies.append(pltpu.make_async_copy(cv_ref.at[page], vbuf.at[slot, :, lanes], sems.at[1, slot]))
        copies.append(pltpu.make_async_copy(cl_ref.at[page], lbuf.at[slot, :, lanes], sems.at[2, slot]))
    return copies


def _dot_f32(x, m01):
    out = None
    for piece in _split3(x):
        t = jnp.dot(piece.astype(bf16), m01, preferred_element_type=f32)
        out = t if out is None else out + t
    return out


N_SIDE_IN = 7
N_SIDE_SCRATCH = 9


def _side_begin(sr, step, *, n_steps, n_chunks, n_pages, g, n_tok, b0):
    (pt_ref, q_ref, kn_ref, vn_ref, lfn_ref, ck_ref, cv_ref, cl_ref, o_ref,
     kbuf, vbuf, lbuf, sems, m_ref, l_ref, acc_ref, run_ref, qbd_ref) = sr
    c = step % n_chunks
    slot = step % 2
    copies = functools.partial(_page_copies, pt_ref, ck_ref, cv_ref, cl_ref, kbuf, vbuf, lbuf, sems,
                               n_chunks=n_chunks, n_pages=n_pages, g=g, b0=b0)
    rows = n_tok * N_HEADS
    contract_last = (((1,), (1,)), ((), ()))

    @pl.when(step == 0)
    def _():
        for cp in copies(step, slot):
            cp.start()

    @pl.when(step + 1 < n_steps)
    def _():
        for cp in copies(step + 1, 1 - slot):
            cp.start()

    @pl.when(c == 0)
    def _():
        lane = lax.broadcasted_iota(jnp.int32, (N_HEADS, D_B), 1)
        head = lax.broadcasted_iota(jnp.int32, (N_HEADS, D_B), 0)
        own = (lane // HEAD_DIM) == head
        q4 = q_ref[0]
        qbd = jnp.concatenate([jnp.where(own, jnp.broadcast_to(q4[t:t + 1, :], (N_HEADS, D_B)), 0.0)
                               for t in range(n_tok)], axis=0)
        qbd_ref[...] = qbd.astype(bf16)
        s = lax.dot_general(qbd.astype(bf16), kn_ref[0].astype(bf16), contract_last,
                            preferred_element_type=f32)
        lfn = lfn_ref[0]
        acc_rows = [lfn[0:1, :]]
        for t in range(1, V7X_SUBLANES):
            acc_rows.append(acc_rows[-1] + lfn[t:t + 1, :])
        dq = jnp.concatenate(acc_rows, axis=0)
        dq_t = dq.T[0:N_HEADS, 0:V7X_SUBLANES]
        bias = jnp.concatenate([dq_t] * n_tok, axis=0)
        r = lax.broadcasted_iota(jnp.int32, (rows, V7X_SUBLANES), 0) // N_HEADS
        t2 = lax.broadcasted_iota(jnp.int32, (rows, V7X_SUBLANES), 1)
        s = jnp.where(t2 <= r, s - bias, NEG_BIG)
        m = jnp.max(s, axis=-1, keepdims=True)
        p = jnp.exp(s - m)
        m_ref[...] = m
        l_ref[...] = jnp.sum(p, axis=-1, keepdims=True)
        acc_ref[...] = jnp.dot(p.astype(bf16), vn_ref[0].astype(bf16), preferred_element_type=f32)
        run_ref[...] = jnp.zeros(run_ref.shape, f32)

    for cp in copies(step, slot):
        cp.wait()
    return slot


def _side_scores(sr, slot, *, g, n_tok):
    (pt_ref, q_ref, kn_ref, vn_ref, lfn_ref, ck_ref, cv_ref, cl_ref, o_ref,
     kbuf, vbuf, lbuf, sems, m_ref, l_ref, acc_ref, run_ref, qbd_ref) = sr
    s = jnp.dot(qbd_ref[...], kbuf[slot].astype(bf16), preferred_element_type=f32)

    lfc = lbuf[slot]
    lfp = jnp.concatenate([lfc[:, j * PAGE_SIZE:(j + 1) * PAGE_SIZE] for j in range(g)], axis=0)
    jj = lax.broadcasted_iota(jnp.int32, (PAGE_SIZE, PAGE_SIZE), 0)
    ss = lax.broadcasted_iota(jnp.int32, (PAGE_SIZE, PAGE_SIZE), 1)
    later = jnp.where(jj > ss, 1.0, 0.0).astype(bf16)
    within = _dot_f32(lfp, later)
    total = jnp.sum(lfp, axis=-1, keepdims=True)
    run = run_ref[:, 0:1]
    tiles = [None] * g
    for j in reversed(range(g)):
        tiles[j] = within[j * N_HEADS:(j + 1) * N_HEADS, :] + run
        run = run + total[j * N_HEADS:(j + 1) * N_HEADS, :]
    run_ref[...] = jnp.broadcast_to(run, run_ref.shape)
    bias = jnp.concatenate(tiles, axis=1)
    return s + jnp.concatenate([bias] * n_tok, axis=0)


def _side_update(sr, s, slot):
    (pt_ref, q_ref, kn_ref, vn_ref, lfn_ref, ck_ref, cv_ref, cl_ref, o_ref,
     kbuf, vbuf, lbuf, sems, m_ref, l_ref, acc_ref, run_ref, qbd_ref) = sr
    contract_last = (((1,), (1,)), ((), ()))
    m_old = m_ref[...]
    m_new = jnp.maximum(m_old, jnp.max(s, axis=-1, keepdims=True))
    alpha = jnp.exp(m_old - m_new)
    p = jnp.exp(s - m_new)
    m_ref[...] = m_new
    l_ref[...] = alpha * l_ref[...] + jnp.sum(p, axis=-1, keepdims=True)
    pv = lax.dot_general(p.astype(bf16), vbuf[slot].astype(bf16), contract_last, preferred_element_type=f32)
    acc_ref[...] = alpha * acc_ref[...] + pv


def _side_end(sr, step, *, n_chunks, n_tok):
    (pt_ref, q_ref, kn_ref, vn_ref, lfn_ref, ck_ref, cv_ref, cl_ref, o_ref,
     kbuf, vbuf, lbuf, sems, m_ref, l_ref, acc_ref, run_ref, qbd_ref) = sr

    @pl.when(step % n_chunks == n_chunks - 1)
    def _():
        lane = lax.broadcasted_iota(jnp.int32, (N_HEADS, D_B), 1)
        head = lax.broadcasted_iota(jnp.int32, (N_HEADS, D_B), 0)
        own = (lane // HEAD_DIM) == head
        o = acc_ref[...] / l_ref[...]
        outs = [jnp.sum(jnp.where(own, o[t * N_HEADS:(t + 1) * N_HEADS, :], 0.0), axis=0, keepdims=True)
                for t in range(n_tok)]
        outs += [jnp.zeros((V7X_SUBLANES - n_tok, D_B), f32)]
        o_ref[0] = jnp.concatenate(outs, axis=0)


def _host_kernel(*refs, body, n_in, n_out, n_tiles, side, **static):
    if side is None:
        body(*refs, n_tiles=n_tiles, midway=lambda: None, **static)
        return
    pt_ref, rest = refs[0], refs[1:]
    host_in, side_in, rest = rest[:n_in], rest[n_in:n_in + N_SIDE_IN], rest[n_in + N_SIDE_IN:]
    host_out, side_out, rest = rest[:n_out], rest[n_out], rest[n_out + 1:]
    host_scratch, side_scratch = rest[:-N_SIDE_SCRATCH], rest[-N_SIDE_SCRATCH:]
    sr = (pt_ref, *side_in, side_out, *side_scratch)
    n_steps = pl.num_programs(0) * n_tiles
    step = pl.program_id(0) * n_tiles + pl.program_id(1)
    slot = _side_begin(sr, step, n_steps=n_steps, **side)
    s = _side_scores(sr, slot, g=side["g"], n_tok=side["n_tok"])
    body(*host_in, *host_out, *host_scratch, n_tiles=n_tiles, midway=functools.partial(_side_update, sr, s, slot),
         **static)
    _side_end(sr, step, n_chunks=side["n_chunks"], n_tok=side["n_tok"])


def _host_call(body, name, *, grid, in_specs, args, out_specs, out_shape, scratch, vmem, static, side=None):
    n_tiles = grid[1]
    kernel = functools.partial(_host_kernel, body=body, n_in=len(in_specs), n_out=len(out_specs), n_tiles=n_tiles,
                               **static)
    if side is None:
        return pl.pallas_call(functools.partial(kernel, side=None), grid=grid, in_specs=in_specs,
                              out_specs=out_specs, out_shape=out_shape, scratch_shapes=scratch,
                              compiler_params=_params(("arbitrary", "arbitrary"), vmem), name=name)(*args)
    b0, page_table, n_tok, side_args = side
    n_pages = page_table.shape[1]
    g = PAGES_PER_STEP
    n_chunks = n_pages // g
    n_side = grid[0] * n_tiles // n_chunks
    assert n_side * n_chunks == grid[0] * n_tiles
    rows = n_tok * N_HEADS
    seq_of = lambda s, t: (s * n_tiles + t) // n_chunks
    tok = lambda c: pl.BlockSpec((1, V7X_SUBLANES, c), lambda s, t, *_: (b0 + seq_of(s, t), 0, 0))
    anyspec = pl.BlockSpec(memory_space=pl.ANY)
    grid_spec = pltpu.PrefetchScalarGridSpec(
        num_scalar_prefetch=1, grid=grid,
        in_specs=list(in_specs) + [tok(D_B), tok(D_B), tok(D_B), tok(V7X_LANES), anyspec, anyspec, anyspec],
        out_specs=list(out_specs) + [pl.BlockSpec((1, V7X_SUBLANES, D_B), lambda s, t, *_: (seq_of(s, t), 0, 0))],
        scratch_shapes=list(scratch) + [
            pltpu.VMEM((2, D_B, g * PAGE_SIZE), f32), pltpu.VMEM((2, D_B, g * PAGE_SIZE), f32),
            pltpu.VMEM((2, N_HEADS, g * PAGE_SIZE), f32), pltpu.SemaphoreType.DMA((3, 2)),
            pltpu.VMEM((rows, 1), f32), pltpu.VMEM((rows, 1), f32), pltpu.VMEM((rows, D_B), f32),
            pltpu.VMEM((N_HEADS, V7X_LANES), f32), pltpu.VMEM((rows, D_B), bf16)])
    side_vmem = 4 * g * PAGE_SIZE * D_B * 4 + 6 * g * PAGE_SIZE * D_B * 2
    cfg = dict(n_chunks=n_chunks, n_pages=n_pages, g=g, n_tok=n_tok, b0=b0)
    return pl.pallas_call(
        functools.partial(kernel, side=cfg), grid_spec=grid_spec,
        out_shape=list(out_shape) + [jax.ShapeDtypeStruct((n_side, V7X_SUBLANES, D_B), f32)],
        compiler_params=_params(("arbitrary", "arbitrary"), vmem + side_vmem), name=name)(
            page_table.reshape(-1), *args, *side_args)


def _out_proj_kernel(x_ref, ya_ref, yb_ref, wa_ref, wb_ref, o_ref):
    y = jnp.dot(ya_ref[...], wa_ref[...], preferred_element_type=f32)
    y = y + jnp.dot(yb_ref[...], wb_ref[...], preferred_element_type=f32)
    o_ref[...] = x_ref[...] + y


def _out_proj(x, ya, yb, wa, wb, *, tm):
    n = x.shape[0]
    row = lambda c: pl.BlockSpec((tm, c), lambda i: (i, 0))
    return pl.pallas_call(
        _out_proj_kernel, grid=(n // tm,),
        in_specs=[row(D_MODEL), row(D_A), row(D_B), _const_spec(wa.shape), _const_spec(wb.shape)],
        out_specs=row(D_MODEL), out_shape=jax.ShapeDtypeStruct((n, D_MODEL), f32),
        compiler_params=_params(("arbitrary",), 32 << 20), name="out_proj")(x, ya, yb, wa, wb)


def _ffn_kernel(*refs, tm, stride, n_tiles, has_hist, midway):
    if has_hist:
        x_ref, g_ref, wu_ref, cw_ref, wd_ref, hist_ref, o_ref, st_ref, ext_ref, a_ref = refs
    else:
        x_ref, g_ref, wu_ref, cw_ref, wd_ref, o_ref, st_ref, ext_ref, a_ref = refs
        hist_ref = None
    xn = _rmsnorm(x_ref[...], g_ref[...]).astype(bf16)
    _conv_init(ext_ref, hist_ref, width=CONV_FFN, stride=stride, tile=pl.program_id(1))
    for c0 in range(0, D_FF, FFN_CHUNK):
        if c0 == D_FF // FFN_CHUNK // 2 * FFN_CHUNK:
            midway()
        cols = slice(c0, c0 + FFN_CHUNK)
        gate = jnp.dot(xn, wu_ref[:, c0:c0 + FFN_CHUNK], preferred_element_type=f32)
        val = jnp.dot(xn, wu_ref[:, D_FF + c0:D_FF + c0 + FFN_CHUNK], preferred_element_type=f32)
        gconv = _causal_conv(ext_ref, gate, cw_ref, st_ref, width=CONV_FFN, stride=stride, tm=tm,
                             n_tiles=n_tiles, cols=cols)
        a_ref[:, cols] = (gconv * _sigmoid(gconv) * val).astype(bf16)
    o_ref[...] = x_ref[...] + jnp.dot(a_ref[...], wd_ref[...], preferred_element_type=f32)


def _ffn(x, g, wu, cw, wd, hist, *, layer, n_seq, n_tiles, tm, stride, side=None):
    n = x.shape[0]
    hp, h = _conv_rows(CONV_FFN, stride)
    row = pl.BlockSpec((tm, D_MODEL), lambda s, t, *_: (s * n_tiles + t, 0))
    slab = lambda a: pl.BlockSpec((None,) + a.shape[1:], lambda *_: (layer, 0, 0), pipeline_mode=pl.Buffered(1))
    in_specs = [row, _const_spec(g.shape), slab(wu), _const_spec(cw.shape), slab(wd)]
    args = [x, g, wu, cw, wd]
    if hist is not None:
        in_specs.append(_const_spec(hist.shape))
        args.append(hist)
    weights = 2 * (wu.size + wd.size) // wu.shape[0]
    vmem = weights + 16 * tm * D_MODEL + (hp + tm) * D_FF * 4 + tm * D_FF * 2 + (12 << 20)
    return _host_call(
        _ffn_kernel, "conv_ffn", grid=(n_seq, n_tiles), in_specs=in_specs, args=args,
        out_specs=[row, pl.BlockSpec((1, h, D_FF), lambda s, t, *_: (s, 0, 0))],
        out_shape=[jax.ShapeDtypeStruct((n, D_MODEL), f32), jax.ShapeDtypeStruct((n_seq, h, D_FF), f32)],
        scratch=[pltpu.VMEM((hp + tm, D_FF), f32), pltpu.VMEM((tm, D_FF), bf16)],
        vmem=vmem, static=dict(tm=tm, stride=stride, has_hist=hist is not None), side=side)


def _conformer_kernel(*refs, tm, stride, n_tiles, has_hist, grouped, midway):
    refs = list(refs)
    group_ref = refs.pop() if grouped else None
    if has_hist:
        (x_ref, g_ref, w1_ref, b1_ref, cw_ref, cb_ref, lg_ref, lb_ref, w2_ref, hist_ref,
         o_ref, st_ref, ext_ref) = refs
    else:
        (x_ref, g_ref, w1_ref, b1_ref, cw_ref, cb_ref, lg_ref, lb_ref, w2_ref,
         o_ref, st_ref, ext_ref) = refs
        hist_ref = None
    xn = _rmsnorm(x_ref[...], g_ref[...]).astype(bf16)
    _conv_init(ext_ref, hist_ref, width=CONV_C, stride=stride, tile=pl.program_id(1))
    parts = []
    for j, c0 in enumerate(range(0, D_C, CONV_CHUNK)):
        if c0 == D_C // 2:
            midway()
        cols = slice(c0, c0 + CONV_CHUNK)
        gcols = slice(D_C + c0, D_C + c0 + CONV_CHUNK)
        a = jnp.dot(xn, w1_ref[:, cols], preferred_element_type=f32) + b1_ref[:, cols]
        g = jnp.dot(xn, w1_ref[:, gcols], preferred_element_type=f32) + b1_ref[:, gcols]
        u = a * _sigmoid(g)
        for i, l0 in enumerate(range(0, CONV_CHUNK, V7X_LANES)):
            lanes = slice(c0 + l0, c0 + l0 + V7X_LANES)
            grp = group_ref.at[j * (CONV_CHUNK // V7X_LANES) + i] if grouped else None
            y = _causal_conv(ext_ref, u[:, l0:l0 + V7X_LANES], cw_ref, st_ref, width=CONV_C, stride=stride, tm=tm,
                             n_tiles=n_tiles, cols=lanes, group_ref=grp)
            parts.append(y + cb_ref[:, lanes])
    y = jnp.concatenate(parts, axis=1)
    mu = jnp.mean(y, axis=-1, keepdims=True)
    yc = y - mu
    yn = yc * lax.rsqrt(jnp.mean(yc * yc, axis=-1, keepdims=True) + EPS) * lg_ref[...] + lb_ref[...]
    a = yn * _sigmoid(yn)
    o_ref[...] = x_ref[...] + jnp.dot(a.astype(bf16), w2_ref[...], preferred_element_type=f32)


def _conformer(x, g, w1, b1, cw, cb, lg, lb, w2, hist, *, n_seq, n_tiles, tm, stride, side=None):
    n = x.shape[0]
    hp, h = _conv_rows(CONV_C, stride)
    grouped = stride % V7X_SUBLANES != 0
    row = pl.BlockSpec((tm, D_MODEL), lambda s, t, *_: (s * n_tiles + t, 0))
    consts = [g, w1, b1, cw, cb, lg, lb, w2] + ([hist] if hist is not None else [])
    scratch = [pltpu.VMEM((hp + tm, D_C), f32)]
    group_bytes = 0
    if grouped:
        group_shape = (D_C // V7X_LANES, V7X_SUBLANES - 1, tm + V7X_SUBLANES, V7X_LANES)
        scratch.append(pltpu.VMEM(group_shape, f32))
        group_bytes = 4 * group_shape[0] * group_shape[1] * group_shape[2] * group_shape[3]
    vmem = 2 * (w1.size + w2.size) + 16 * tm * D_MODEL + 6 * (hp + tm) * D_C * 4 + group_bytes + (8 << 20)
    return _host_call(
        _conformer_kernel, "conformer", grid=(n_seq, n_tiles),
        in_specs=[row] + [_const_spec(a.shape) for a in consts], args=[x] + consts,
        out_specs=[row, pl.BlockSpec((1, h, D_C), lambda s, t, *_: (s, 0, 0))],
        out_shape=[jax.ShapeDtypeStruct((n, D_MODEL), f32), jax.ShapeDtypeStruct((n_seq, h, D_C), f32)],
        scratch=scratch, vmem=vmem,
        static=dict(tm=tm, stride=stride, has_hist=hist is not None, grouped=grouped), side=side)


def _time_major(a):
    return jnp.swapaxes(a, 0, 1).reshape(-1, a.shape[-1])


def _batch_major(a, n_batch):
    return jnp.swapaxes(a.reshape(-1, n_batch, a.shape[-1]), 0, 1)


def _pad_tokens(a):
    return jnp.pad(a, ((0, 0), (0, V7X_SUBLANES - a.shape[1]), (0, 0)))


def _heads_last(a_t, n_seq, seq):
    return jnp.transpose(a_t.reshape(n_seq, N_HEADS, HEAD_DIM, seq), (0, 3, 1, 2))


def kernel(x_prompt, x_sample, cache_k, cache_v, cache_logf, page_table, state_conv_a, state_conv_c, state_ffn, norm_mix0, w_in0, b_forget, conv_a_w, q_norm, k_norm, w_out0, norm_mix1, w_pw1, b_pw1, conv_c_w, conv_c_b, ln_c_g, ln_c_b, w_pw2, norm_ffn, w_up, conv_ffn_w, w_down):
    n_batch, seq, _ = x_prompt.shape
    dec_batch, dec_seq, _ = x_sample.shape
    n_pool = cache_k.shape[1]
    p_tiles = seq // PROMPT_TM
    prompt = dict(n_seq=n_batch, n_tiles=p_tiles, tm=PROMPT_TM, stride=1)
    sample = dict(n_seq=1, n_tiles=1, tm=dec_batch * dec_seq, stride=dec_batch)
    vec = lambda a: a.reshape(1, -1)

    yp = x_prompt.reshape(n_batch * seq, D_MODEL)
    ys = _time_major(x_sample)

    w_in = jnp.pad(w_in0[0], ((0, 0), (0, D_IN_PAD - w_in0.shape[2]))).astype(bf16)
    b_f = jnp.pad(b_forget[0], (0, V7X_LANES - N_HEADS)).reshape(1, V7X_LANES)
    q_g = vec(jnp.tile(q_norm[0], N_HEADS))
    k_g = vec(jnp.tile(k_norm[0], N_HEADS))
    lane_head = jnp.arange(D_B) // HEAD_DIM
    head_mean = jnp.where(lane_head[:, None] == lane_head[None, :], 1.0 / HEAD_DIM, 0.0).astype(bf16)
    w_oa = w_out0[0, :D_A].astype(bf16)
    w_ob = w_out0[0, D_A:].astype(bf16)
    proj = (vec(norm_mix0[0]), w_in, b_f, conv_a_w[0], q_g, k_g, head_mean)

    ya_s, q_ts, k_s, _, v_ts, _, lf_s, _, sca = _in_proj(ys, *proj, _time_major(state_conv_a[0]), **sample)
    to_batch = lambda a: _batch_major(a, dec_batch)
    q_s = to_batch(q_ts[0].T)
    v_s = to_batch(v_ts[0].T)
    side_args = (_pad_tokens(q_s), _pad_tokens(to_batch(k_s)), _pad_tokens(v_s), _pad_tokens(to_batch(lf_s)),
                 jnp.transpose(cache_k[0], (0, 2, 3, 1)).reshape(n_pool, D_B, PAGE_SIZE),
                 jnp.transpose(cache_v[0], (0, 2, 3, 1)).reshape(n_pool, D_B, PAGE_SIZE),
                 jnp.swapaxes(cache_logf[0], 1, 2))
    per_host = n_batch * p_tiles // (page_table.shape[1] // PAGES_PER_STEP)
    assert 4 * per_host == dec_batch
    side = lambda i: (i * per_host, page_table, dec_seq, side_args)
    yb_parts = []

    ya, q_t, k, k_t, v_t, vb_t, lf, lf_t, pca, part = _in_proj(yp, *proj, None, **prompt, side=side(0))
    yb_parts.append(part)
    kt, qtail = _fox_prep(lf, k, n_seq=n_batch, seq=seq)
    yb = _fox_attn(q_t, qtail, kt, vb_t, n_seq=n_batch, seq=seq)
    yp = _out_proj(yp, ya, yb.reshape(n_batch * seq, D_B), w_oa, w_ob, tm=WIDE_TM)
    p_k = _heads_last(k_t, n_batch, seq)[None]
    p_v = _heads_last(v_t, n_batch, seq)[None]
    p_logf = jnp.swapaxes(lf_t, 1, 2)[None]

    wu = w_up.astype(bf16)
    wd = w_down.astype(bf16)
    conf = (vec(norm_mix1[0]), w_pw1[0].astype(bf16), vec(b_pw1[0]), conv_c_w[0], vec(conv_c_b[0]),
            vec(ln_c_g[0]), vec(ln_c_b[0]), w_pw2[0].astype(bf16))
    p_ffn = []
    for layer in range(2):
        if layer == 1:
            yp, pcc, part = _conformer(yp, *conf, None, **prompt, side=side(2))
            yb_parts.append(part)
        yp, pst, part = _ffn(yp, vec(norm_ffn[layer]), wu, conv_ffn_w[layer], wd, None, layer=layer, **prompt,
                             side=side(1 + 2 * layer))
        yb_parts.append(part)
        p_ffn.append(pst)

    yb_s = _time_major(jnp.concatenate(yb_parts, axis=0)[:, :dec_seq]).astype(bf16)
    ys = _out_proj(ys, ya_s, yb_s, w_oa, w_ob, tm=dec_batch * dec_seq)
    s_k = to_batch(k_s).reshape(1, dec_batch, dec_seq, N_HEADS, HEAD_DIM)
    s_v = v_s.reshape(1, dec_batch, dec_seq, N_HEADS, HEAD_DIM)
    s_logf = to_batch(lf_s)[:, :, :N_HEADS].reshape(1, dec_batch, dec_seq, N_HEADS)
    s_conv_a = to_batch(sca[0])[None]
    s_ffn = []
    for layer in range(2):
        if layer == 1:
            ys, scc = _conformer(ys, *conf, _time_major(state_conv_c[0]), **sample)
        ys, sst = _ffn(ys, vec(norm_ffn[layer]), wu, conv_ffn_w[layer], wd, _time_major(state_ffn[layer]),
                       layer=layer, **sample)
        s_ffn.append(to_batch(sst[0]))

    return (yp.reshape(n_batch, seq, D_MODEL), to_batch(ys),
            pca[None], s_conv_a,
            p_k, s_k, p_v, s_v, p_logf, s_logf,
            pcc[None], to_batch(scc[0])[None],
            jnp.stack(p_ffn), jnp.stack(s_ffn))
```

```python
import functools

import jax
import jax.numpy as jnp
from jax import lax
from jax.experimental import pallas as pl
from jax.experimental.pallas import tpu as pltpu

D_MODEL = 1024
HEAD_DIM = 64
N_HEADS = 8
D_A = 512
D_B = 512
D_C = 1024
D_FF = 2816
CONV_A = 3
CONV_C = 31
CONV_FFN = 3
PAGE_SIZE = 128
EPS = 1e-6

V7X_LANES = 128
V7X_SUBLANES = 8
V7X_VMEM_BYTES = 64 * 1024 * 1024
NEG_BIG = -1e30
LOG2E = 1.4426950408889634

D_IN_PAD = 3 * D_A + 3 * D_B + V7X_LANES
BIAS_LANE = HEAD_DIM
TAIL_ROWS = V7X_SUBLANES
VT_ROWS = HEAD_DIM + 16

PROMPT_TM = 256
FFN_CHUNK = 256
CONV_CHUNK = 256
PREP_TS = 1024
PAGES_PER_STEP = 16

bf16 = jnp.bfloat16
f32 = jnp.float32


def _round_up(n, m):
    return (n + m - 1) // m * m


def _const_spec(shape):
    zeros = (0,) * len(shape)
    return pl.BlockSpec(shape, lambda *_: zeros, pipeline_mode=pl.Buffered(1))


def _params(semantics, vmem_bytes):
    return pltpu.CompilerParams(dimension_semantics=semantics,
                                vmem_limit_bytes=min(int(vmem_bytes), V7X_VMEM_BYTES - (4 << 20)))


def _rmsnorm(x, g):
    return x * lax.rsqrt(jnp.mean(x * x, axis=-1, keepdims=True) + EPS) * g


def _sigmoid(x):
    return 1.0 / (1.0 + jnp.exp(-x))


def _conv_rows(width, stride):
    h = (width - 1) * stride
    return _round_up(h, V7X_SUBLANES), h


def _conv_init(ext_ref, hist_ref, *, width, stride, tile):
    hp, h = _conv_rows(width, stride)

    @pl.when(tile == 0)
    def _():
        if hist_ref is None:
            ext_ref[0:hp, :] = jnp.zeros((hp, ext_ref.shape[1]), f32)
        else:
            ext_ref[hp - h:hp, :] = hist_ref[...]


def _causal_conv(ext_ref, u, w_ref, state_ref, *, width, stride, tm, n_tiles, cols=slice(None), group_ref=None):
    hp, h = _conv_rows(width, stride)
    ext_ref[hp:hp + tm, cols] = u
    starts = [hp - (width - 1 - k) * stride for k in range(width)]
    y = None
    if group_ref is None:
        for k, start in enumerate(starts):
            term = w_ref[k:k + 1, cols] * ext_ref[start:start + tm, cols]
            y = term if y is None else y + term
    else:
        for off in range(V7X_SUBLANES):
            taps = [k for k, start in enumerate(starts) if start % V7X_SUBLANES == off]
            if not taps:
                continue
            rows = tm if off == 0 else tm + V7X_SUBLANES
            part = None
            for k in taps:
                base = starts[k] - off
                term = w_ref[k:k + 1, cols] * ext_ref[base:base + rows, cols]
                part = term if part is None else part + term
            if off:
                group_ref[off - 1, 0:rows, :] = part
                part = group_ref[off - 1, off:off + tm, :]
            y = part if y is None else y + part
    state_ref[0, :, cols] = ext_ref[hp + tm - h:hp + tm, cols]
    if n_tiles > 1:
        ext_ref[0:hp, cols] = ext_ref[tm:tm + hp, cols]
    return y


def _in_proj_kernel(*refs, tm, stride, n_tiles, has_hist, midway):
    if has_hist:
        (x_ref, g_ref, w_ref, bf_ref, cw_ref, qg_ref, kg_ref, bd_ref, hist_ref,
         ya_ref, qt_ref, k_ref, kt_ref, vt_ref, vbt_ref, lf_ref, lft_ref, st_ref, ext_ref) = refs
    else:
        (x_ref, g_ref, w_ref, bf_ref, cw_ref, qg_ref, kg_ref, bd_ref,
         ya_ref, qt_ref, k_ref, kt_ref, vt_ref, vbt_ref, lf_ref, lft_ref, st_ref, ext_ref) = refs
        hist_ref = None
    tile = pl.program_id(1)
    xn = _rmsnorm(x_ref[...], g_ref[...]).astype(bf16)
    _conv_init(ext_ref, hist_ref, width=CONV_A, stride=stride, tile=tile)
    h = jnp.dot(xn, w_ref[:, 3 * D_A:], preferred_element_type=f32)
    q = h[:, 0:D_B]
    k = h[:, D_B:2 * D_B]
    v = h[:, 2 * D_B:3 * D_B]
    fl = h[:, 3 * D_B:]

    def head_norm(z, gain):
        ms = jnp.dot((z * z).astype(bf16), bd_ref[...], preferred_element_type=f32)
        return z * lax.rsqrt(ms + EPS) * gain

    qn = head_norm(q, qg_ref[...]) * (HEAD_DIM ** -0.5)
    kn = head_norm(k, kg_ref[...])
    midway()
    ha = jnp.dot(xn, w_ref[:, 0:3 * D_A], preferred_element_type=f32)
    u = ha[:, D_A:2 * D_A] * ha[:, 2 * D_A:3 * D_A]
    yc = _causal_conv(ext_ref, u, cw_ref, st_ref, width=CONV_A, stride=stride, tm=tm, n_tiles=n_tiles)
    ya_ref[...] = (ha[:, 0:D_A] * yc).astype(bf16)

    qt_ref[0] = qn.T
    k_ref[...] = kn
    kt_ref[0] = kn.T
    vt = v.T
    vt_ref[0] = vt
    ones = jnp.ones((VT_ROWS - HEAD_DIM, vt.shape[1]), f32)
    vbt_ref[0, 0] = jnp.concatenate(
        [blk for hd in range(N_HEADS) for blk in (vt[hd * HEAD_DIM:(hd + 1) * HEAD_DIM, :], ones)], axis=0).astype(bf16)
    z = fl + bf_ref[...]
    logf = jnp.minimum(z, 0.0) - jnp.log(1.0 + jnp.exp(-jnp.abs(z)))
    lane = lax.broadcasted_iota(jnp.int32, logf.shape, 1)
    logf = jnp.where(lane < N_HEADS, logf, 0.0)
    lf_ref[...] = logf
    lft_ref[0] = logf.T[0:N_HEADS, :]


def _in_proj(x, g, w, bfv, cw, qg, kg, bd, hist, *, n_seq, n_tiles, tm, stride, side=None):
    n = x.shape[0]
    seq = n_tiles * tm
    hp, h = _conv_rows(CONV_A, stride)
    row = lambda c: pl.BlockSpec((tm, c), lambda s, t, *_: (s * n_tiles + t, 0))
    col = lambda r: pl.BlockSpec((1, r, tm), lambda s, t, *_: (s, 0, t))
    in_specs = [row(D_MODEL), _const_spec(g.shape), _const_spec(w.shape), _const_spec(bfv.shape),
                _const_spec(cw.shape), _const_spec(qg.shape), _const_spec(kg.shape), _const_spec(bd.shape)]
    args = [x, g, w, bfv, cw, qg, kg, bd]
    if hist is not None:
        in_specs.append(_const_spec(hist.shape))
        args.append(hist)
    out_shape = [jax.ShapeDtypeStruct((n, D_A), bf16),
                 jax.ShapeDtypeStruct((n_seq, D_B, seq), f32),
                 jax.ShapeDtypeStruct((n, D_B), f32),
                 jax.ShapeDtypeStruct((n_seq, D_B, seq), f32),
                 jax.ShapeDtypeStruct((n_seq, D_B, seq), f32),
                 jax.ShapeDtypeStruct((n_seq, n_tiles, N_HEADS * VT_ROWS, tm), bf16),
                 jax.ShapeDtypeStruct((n, V7X_LANES), f32),
                 jax.ShapeDtypeStruct((n_seq, N_HEADS, seq), f32),
                 jax.ShapeDtypeStruct((n_seq, h, D_A), f32)]
    out_specs = [row(D_A), col(D_B), row(D_B), col(D_B), col(D_B),
                 pl.BlockSpec((1, 1, N_HEADS * VT_ROWS, tm), lambda s, t, *_: (s, t, 0, 0)),
                 row(V7X_LANES), col(N_HEADS),
                 pl.BlockSpec((1, h, D_A), lambda s, t, *_: (s, 0, 0))]
    vmem = 2 * w.size + 10 * tm * D_IN_PAD * 4 + (8 << 20)
    return _host_call(_in_proj_kernel, "in_proj", grid=(n_seq, n_tiles), in_specs=in_specs, args=args,
                      out_specs=out_specs, out_shape=out_shape, scratch=[pltpu.VMEM((hp + tm, D_A), f32)],
                      vmem=vmem, static=dict(tm=tm, stride=stride, has_hist=hist is not None), side=side)


def _split3(x):
    hi = x.astype(bf16).astype(f32)
    r = x - hi
    mid = r.astype(bf16).astype(f32)
    return hi, mid, r - mid


def _fox_prep_kernel(lf_ref, k_ref, kt_ref, qtail_ref, carry_ref, *, ts):
    @pl.when(pl.program_id(1) == 0)
    def _():
        carry_ref[...] = jnp.zeros(carry_ref.shape, f32)

    c = lf_ref[...]
    row = lax.broadcasted_iota(jnp.int32, c.shape, 0)
    shift = 1
    while shift < ts:
        c = c + jnp.where(row >= shift, pltpu.roll(c, shift, axis=0), 0.0)
        shift *= 2
    c = c + carry_ref[0:1, :]
    carry_ref[...] = jnp.broadcast_to(c[ts - 1:ts, :], carry_ref.shape)
    c = c * LOG2E
    c_t = c.T

    lane = lax.broadcasted_iota(jnp.int32, (ts, V7X_LANES), 1)
    ones = jnp.where((lane >= BIAS_LANE) & (lane < BIAS_LANE + 3), 1.0, 0.0)
    one_row = jnp.ones((1, ts), f32)
    zero_row = jnp.zeros((1, ts), f32)
    for hd in range(N_HEADS):
        hi, mid, lo = _split3(jnp.broadcast_to(c[:, hd:hd + 1], (ts, V7X_LANES)))
        neg = jnp.where(lane == BIAS_LANE + 3, -hi, jnp.where(lane == BIAS_LANE + 4, -mid,
                        jnp.where(lane == BIAS_LANE + 5, -lo, ones)))
        g0 = (hd // 2) * V7X_LANES
        ks = k_ref[:, g0:g0 + V7X_LANES]
        if hd % 2:
            ks = pltpu.roll(ks, HEAD_DIM, axis=1)
        kt_ref[0, hd] = jnp.where(lane < HEAD_DIM, ks, neg).astype(bf16)
        hi_t, mid_t, lo_t = _split3(c_t[hd:hd + 1, :])
        qtail_ref[0, hd] = jnp.concatenate([hi_t, mid_t, lo_t, one_row, one_row, one_row, zero_row, zero_row],
                                           axis=0)


def _fox_prep(lf, k, *, n_seq, seq):
    ts = PREP_TS
    n_tiles = seq // ts
    row = lambda c: pl.BlockSpec((ts, c), lambda s, t, *_: (s * n_tiles + t, 0))
    return pl.pallas_call(
        functools.partial(_fox_prep_kernel, ts=ts),
        grid=(n_seq, n_tiles), in_specs=[row(V7X_LANES), row(D_B)],
        out_specs=[pl.BlockSpec((1, N_HEADS, ts, V7X_LANES), lambda s, t: (s, 0, t, 0)),
                   pl.BlockSpec((1, N_HEADS, TAIL_ROWS, ts), lambda s, t: (s, 0, 0, t))],
        out_shape=[jax.ShapeDtypeStruct((n_seq, N_HEADS, seq, V7X_LANES), bf16),
                   jax.ShapeDtypeStruct((n_seq, N_HEADS, TAIL_ROWS, seq), f32)],
        scratch_shapes=[pltpu.VMEM((V7X_SUBLANES, V7X_LANES), f32)],
        compiler_params=_params(("arbitrary", "arbitrary"), 40 << 20), name="fox_prep")(lf, k)


def _fox_attn_kernel(qt_ref, qtail_ref, kt_ref, vbt_ref, o_ref, qa_ref, m_ref, acc_ref, sa_ref, sb_ref, *, tq):
    qi = pl.program_id(1)
    pad = jnp.zeros((V7X_LANES - HEAD_DIM - TAIL_ROWS, tq), f32)
    for hd in range(N_HEADS):
        qa = jnp.concatenate([qt_ref[0, hd * HEAD_DIM:(hd + 1) * HEAD_DIM, :] * LOG2E, qtail_ref[0, hd], pad], axis=0)
        qa_ref[hd] = qa.astype(bf16)
    m_ref[...] = jnp.full(m_ref.shape, NEG_BIG, f32)
    acc_ref[...] = jnp.zeros(acc_ref.shape, f32)

    def scores_into(s_ref, j):
        ks = pl.multiple_of(j * tq, tq)
        for hd in range(N_HEADS):
            s_ref[hd] = jnp.dot(kt_ref[0, hd, pl.ds(ks, tq), :], qa_ref[hd],
                                preferred_element_type=f32)

    def consume(s_ref, j, diagonal):
        probs, alphas = [], []
        for hd in range(N_HEADS):
            s = s_ref[hd]
            if diagonal:
                key = lax.broadcasted_iota(jnp.int32, s.shape, 0)
                qry = lax.broadcasted_iota(jnp.int32, s.shape, 1)
                s = jnp.where(key <= qry, s, NEG_BIG)
            m_old = m_ref[hd]
            m_new = jnp.maximum(m_old, jnp.max(s, axis=0, keepdims=True))
            alpha = jnp.exp2(m_old - m_new)
            p = jnp.exp2(s - m_new)
            m_ref[hd] = m_new
            probs.append(p.astype(bf16))
            alphas.append(alpha)
        for hd in range(N_HEADS):
            vv = vbt_ref[0, j, hd * VT_ROWS:(hd + 1) * VT_ROWS, :]
            acc_ref[hd] = alphas[hd] * acc_ref[hd] + jnp.dot(vv, probs[hd], preferred_element_type=f32)

    scores_into(sa_ref, 0)

    def body(i, carry):
        j = 2 * i
        scores_into(sb_ref, j + 1)
        consume(sa_ref, j, False)
        scores_into(sa_ref, j + 2)
        consume(sb_ref, j + 1, False)
        return carry

    lax.fori_loop(0, qi // 2, body, 0)

    @pl.when(qi % 2 == 0)
    def _():
        consume(sa_ref, qi, True)

    @pl.when(qi % 2 == 1)
    def _():
        scores_into(sb_ref, qi)
        consume(sa_ref, qi - 1, False)
        consume(sb_ref, qi, True)

    o = jnp.concatenate([acc_ref[hd, 0:HEAD_DIM, :] / acc_ref[hd, HEAD_DIM:HEAD_DIM + 1, :]
                         for hd in range(N_HEADS)], axis=0)
    o_ref[0] = o.T.astype(o_ref.dtype)


def _fox_attn(qt, qtail, kt, vbt, *, n_seq, seq):
    tq = PROMPT_TM
    return pl.pallas_call(
        functools.partial(_fox_attn_kernel, tq=tq),
        grid=(n_seq, seq // tq),
        in_specs=[pl.BlockSpec((1, D_B, tq), lambda b, i: (b, 0, i)),
                  pl.BlockSpec((1, N_HEADS, TAIL_ROWS, tq), lambda b, i: (b, 0, 0, i)),
                  pl.BlockSpec((1, N_HEADS, seq, V7X_LANES), lambda b, i: (b, 0, 0, 0)),
                  pl.BlockSpec((1, seq // tq, N_HEADS * VT_ROWS, tq), lambda b, i: (b, 0, 0, 0))],
        out_specs=pl.BlockSpec((1, tq, D_B), lambda b, i: (b, i, 0)),
        out_shape=jax.ShapeDtypeStruct((n_seq, seq, D_B), bf16),
        scratch_shapes=[pltpu.VMEM((N_HEADS, V7X_LANES, tq), bf16), pltpu.VMEM((N_HEADS, 1, tq), f32),
                        pltpu.VMEM((N_HEADS, VT_ROWS, tq), f32),
                        pltpu.VMEM((N_HEADS, tq, tq), f32), pltpu.VMEM((N_HEADS, tq, tq), f32)],
        compiler_params=_params(("arbitrary", "arbitrary"), 48 << 20), name="fox_attn")(
            qt, qtail, kt, vbt)


def _page_copies(pt_ref, ck_ref, cv_ref, cl_ref, kbuf, vbuf, lbuf, sems, step, slot, *, n_chunks, n_pages, g, b0):
    b = b0 + step // n_chunks
    c = step % n_chunks
    base = b * n_pages + (n_chunks - 1 - c) * g
    copies = []
    for j in range(g):
        page = pt_ref[base + j]
        lanes = pl.ds(j * PAGE_SIZE, PAGE_SIZE)
        copies.append(pltpu.make_async_copy(ck_ref.at[page], kbuf.at[slot, :, lanes], sems.at[0, slot]))
        copies.append(pltpu.make_async_copy(cv_ref.at[page], vbuf.at[slot, :, lanes], sems.at[1, slot]))
        copies.append(pltpu.make_async_copy(cl_ref.at[page], lbuf.at[slot, :, lanes], sems.at[2, slot]))
    return copies


def _dot_f32(x, m01):
    out = None
    for piece in _split3(x):
        t = jnp.dot(piece.astype(bf16), m01, preferred_element_type=f32)
        out = t if out is None else out + t
    return out


N_SIDE_IN = 7
N_SIDE_SCRATCH = 9


def _side_begin(sr, step, *, n_steps, n_chunks, n_pages, g, n_tok, b0):
    (pt_ref, q_ref, kn_ref, vn_ref, lfn_ref, ck_ref, cv_ref, cl_ref, o_ref,
     kbuf, vbuf, lbuf, sems, m_ref, l_ref, acc_ref, run_ref, qbd_ref) = sr
    c = step % n_chunks
    slot = step % 2
    copies = functools.partial(_page_copies, pt_ref, ck_ref, cv_ref, cl_ref, kbuf, vbuf, lbuf, sems,
                               n_chunks=n_chunks, n_pages=n_pages, g=g, b0=b0)
    rows = n_tok * N_HEADS
    contract_last = (((1,), (1,)), ((), ()))

    @pl.when(step + 1 < n_steps)
    def _():
        for cp in copies(step + 1, 1 - slot):
            cp.start()

    @pl.when(c == 0)
    def _():
        @pl.when(step == 0)
        def _():
            for cp in copies(step, slot):
                cp.start()

        lane = lax.broadcasted_iota(jnp.int32, (N_HEADS, D_B), 1)
        head = lax.broadcasted_iota(jnp.int32, (N_HEADS, D_B), 0)
        own = (lane // HEAD_DIM) == head
        q4 = q_ref[0]
        qbd = jnp.concatenate([jnp.where(own, jnp.broadcast_to(q4[t:t + 1, :], (N_HEADS, D_B)), 0.0)
                               for t in range(n_tok)], axis=0)
        qbd_ref[...] = qbd.astype(bf16)
        s = lax.dot_general(qbd.astype(bf16), kn_ref[0].astype(bf16), contract_last,
                            preferred_element_type=f32)
        lfn = lfn_ref[0]
        acc_rows = [lfn[0:1, :]]
        for t in range(1, V7X_SUBLANES):
            acc_rows.append(acc_rows[-1] + lfn[t:t + 1, :])
        dq = jnp.concatenate(acc_rows, axis=0)
        dq_t = dq.T[0:N_HEADS, 0:V7X_SUBLANES]
        bias = jnp.concatenate([dq_t] * n_tok, axis=0)
        r = lax.broadcasted_iota(jnp.int32, (rows, V7X_SUBLANES), 0) // N_HEADS
        t2 = lax.broadcasted_iota(jnp.int32, (rows, V7X_SUBLANES), 1)
        s = jnp.where(t2 <= r, s - bias, NEG_BIG)
        m = jnp.max(s, axis=-1, keepdims=True)
        p = jnp.exp(s - m)
        m_ref[...] = m
        l_ref[...] = jnp.sum(p, axis=-1, keepdims=True)
        acc_ref[...] = jnp.dot(p.astype(bf16), vn_ref[0].astype(bf16), preferred_element_type=f32)
        run_ref[...] = jnp.zeros(run_ref.shape, f32)

    for cp in copies(step, slot):
        cp.wait()
    return slot


def _side_scores(sr, slot, *, g, n_tok):
    (pt_ref, q_ref, kn_ref, vn_ref, lfn_ref, ck_ref, cv_ref, cl_ref, o_ref,
     kbuf, vbuf, lbuf, sems, m_ref, l_ref, acc_ref, run_ref, qbd_ref) = sr
    s = jnp.dot(qbd_ref[...], kbuf[slot].astype(bf16), preferred_element_type=f32)

    lfc = lbuf[slot]
    lfp = jnp.concatenate([lfc[:, j * PAGE_SIZE:(j + 1) * PAGE_SIZE] for j in range(g)], axis=0)
    jj = lax.broadcasted_iota(jnp.int32, (PAGE_SIZE, PAGE_SIZE), 0)
    ss = lax.broadcasted_iota(jnp.int32, (PAGE_SIZE, PAGE_SIZE), 1)
    later = jnp.where(jj > ss, 1.0, 0.0).astype(bf16)
    within = _dot_f32(lfp, later)
    total = jnp.sum(lfp, axis=-1, keepdims=True)
    run = run_ref[:, 0:1]
    tiles = [None] * g
    for j in reversed(range(g)):
        tiles[j] = within[j * N_HEADS:(j + 1) * N_HEADS, :] + run
        run = run + total[j * N_HEADS:(j + 1) * N_HEADS, :]
    run_ref[...] = jnp.broadcast_to(run, run_ref.shape)
    bias = jnp.concatenate(tiles, axis=1)
    return s + jnp.concatenate([bias] * n_tok, axis=0)


def _side_update(sr, s, slot):
    (pt_ref, q_ref, kn_ref, vn_ref, lfn_ref, ck_ref, cv_ref, cl_ref, o_ref,
     kbuf, vbuf, lbuf, sems, m_ref, l_ref, acc_ref, run_ref, qbd_ref) = sr
    contract_last = (((1,), (1,)), ((), ()))
    m_old = m_ref[...]
    m_new = jnp.maximum(m_old, jnp.max(s, axis=-1, keepdims=True))
    alpha = jnp.exp(m_old - m_new)
    p = jnp.exp(s - m_new)
    m_ref[...] = m_new
    l_ref[...] = alpha * l_ref[...] + jnp.sum(p, axis=-1, keepdims=True)
    pv = lax.dot_general(p.astype(bf16), vbuf[slot].astype(bf16), contract_last, preferred_element_type=f32)
    acc_ref[...] = alpha * acc_ref[...] + pv


def _side_end(sr, step, *, n_chunks, n_tok):
    (pt_ref, q_ref, kn_ref, vn_ref, lfn_ref, ck_ref, cv_ref, cl_ref, o_ref,
     kbuf, vbuf, lbuf, sems, m_ref, l_ref, acc_ref, run_ref, qbd_ref) = sr

    @pl.when(step % n_chunks == n_chunks - 1)
    def _():
        lane = lax.broadcasted_iota(jnp.int32, (N_HEADS, D_B), 1)
        head = lax.broadcasted_iota(jnp.int32, (N_HEADS, D_B), 0)
        own = (lane // HEAD_DIM) == head
        o = acc_ref[...] / l_ref[...]
        outs = [jnp.sum(jnp.where(own, o[t * N_HEADS:(t + 1) * N_HEADS, :], 0.0), axis=0, keepdims=True)
                for t in range(n_tok)]
        outs += [jnp.zeros((V7X_SUBLANES - n_tok, D_B), f32)]
        o_ref[0] = jnp.concatenate(outs, axis=0)


def _host_kernel(*refs, body, n_in, n_out, n_tiles, side, **static):
    if side is None:
        body(*refs, n_tiles=n_tiles, midway=lambda: None, **static)
        return
    pt_ref, rest = refs[0], refs[1:]
    host_in, side_in, rest = rest[:n_in], rest[n_in:n_in + N_SIDE_IN], rest[n_in + N_SIDE_IN:]
    host_out, side_out, rest = rest[:n_out], rest[n_out], rest[n_out + 1:]
    host_scratch, side_scratch = rest[:-N_SIDE_SCRATCH], rest[-N_SIDE_SCRATCH:]
    sr = (pt_ref, *side_in, side_out, *side_scratch)
    n_steps = pl.num_programs(0) * n_tiles
    step = pl.program_id(0) * n_tiles + pl.program_id(1)
    slot = _side_begin(sr, step, n_steps=n_steps, **side)
    s = _side_scores(sr, slot, g=side["g"], n_tok=side["n_tok"])
    body(*host_in, *host_out, *host_scratch, n_tiles=n_tiles, midway=functools.partial(_side_update, sr, s, slot),
         **static)
    _side_end(sr, step, n_chunks=side["n_chunks"], n_tok=side["n_tok"])


def _host_call(body, name, *, grid, in_specs, args, out_specs, out_shape, scratch, vmem, static, side=None):
    n_tiles = grid[1]
    kernel = functools.partial(_host_kernel, body=body, n_in=len(in_specs), n_out=len(out_specs), n_tiles=n_tiles,
                               **static)
    if side is None:
        return pl.pallas_call(functools.partial(kernel, side=None), grid=grid, in_specs=in_specs,
                              out_specs=out_specs, out_shape=out_shape, scratch_shapes=scratch,
                              compiler_params=_params(("arbitrary", "arbitrary"), vmem), name=name)(*args)
    b0, page_table, n_tok, side_args = side
    n_pages = page_table.shape[1]
    g = PAGES_PER_STEP
    n_chunks = n_pages // g
    n_side = grid[0] * n_tiles // n_chunks
    assert n_side * n_chunks == grid[0] * n_tiles
    rows = n_tok * N_HEADS
    seq_of = lambda s, t: (s * n_tiles + t) // n_chunks
    tok = lambda c: pl.BlockSpec((1, V7X_SUBLANES, c), lambda s, t, *_: (b0 + seq_of(s, t), 0, 0))
    anyspec = pl.BlockSpec(memory_space=pl.ANY)
    grid_spec = pltpu.PrefetchScalarGridSpec(
        num_scalar_prefetch=1, grid=grid,
        in_specs=list(in_specs) + [tok(D_B), tok(D_B), tok(D_B), tok(V7X_LANES), anyspec, anyspec, anyspec],
        out_specs=list(out_specs) + [pl.BlockSpec((1, V7X_SUBLANES, D_B), lambda s, t, *_: (seq_of(s, t), 0, 0))],
        scratch_shapes=list(scratch) + [
            pltpu.VMEM((2, D_B, g * PAGE_SIZE), f32), pltpu.VMEM((2, D_B, g * PAGE_SIZE), f32),
            pltpu.VMEM((2, N_HEADS, g * PAGE_SIZE), f32), pltpu.SemaphoreType.DMA((3, 2)),
            pltpu.VMEM((rows, 1), f32), pltpu.VMEM((rows, 1), f32), pltpu.VMEM((rows, D_B), f32),
            pltpu.VMEM((N_HEADS, V7X_LANES), f32), pltpu.VMEM((rows, D_B), bf16)])
    side_vmem = 4 * g * PAGE_SIZE * D_B * 4 + 6 * g * PAGE_SIZE * D_B * 2
    cfg = dict(n_chunks=n_chunks, n_pages=n_pages, g=g, n_tok=n_tok, b0=b0)
    return pl.pallas_call(
        functools.partial(kernel, side=cfg), grid_spec=grid_spec,
        out_shape=list(out_shape) + [jax.ShapeDtypeStruct((n_side, V7X_SUBLANES, D_B), f32)],
        compiler_params=_params(("arbitrary", "arbitrary"), vmem + side_vmem), name=name)(
            page_table.reshape(-1), *args, *side_args)


def _out_proj_kernel(x_ref, ya_ref, yb_ref, wa_ref, wb_ref, o_ref):
    y = jnp.dot(ya_ref[...], wa_ref[...], preferred_element_type=f32)
    y = y + jnp.dot(yb_ref[...], wb_ref[...], preferred_element_type=f32)
    o_ref[...] = x_ref[...] + y


def _out_proj(x, ya, yb, wa, wb, *, tm):
    n = x.shape[0]
    row = lambda c: pl.BlockSpec((tm, c), lambda i: (i, 0))
    return pl.pallas_call(
        _out_proj_kernel, grid=(n // tm,),
        in_specs=[row(D_MODEL), row(D_A), row(D_B), _const_spec(wa.shape), _const_spec(wb.shape)],
        out_specs=row(D_MODEL), out_shape=jax.ShapeDtypeStruct((n, D_MODEL), f32),
        compiler_params=_params(("arbitrary",), 32 << 20), name="out_proj")(x, ya, yb, wa, wb)


def _ffn_kernel(*refs, tm, stride, n_tiles, has_hist, has_mix, midway):
    refs = list(refs)
    if has_mix:
        xs_ref = refs.pop()
        ya_ref, yb_ref, wa_ref, wb_ref = refs[5 + has_hist:9 + has_hist]
        del refs[5 + has_hist:9 + has_hist]
    if has_hist:
        x_ref, g_ref, wu_ref, cw_ref, wd_ref, hist_ref, o_ref, st_ref, ext_ref, a_ref = refs
    else:
        x_ref, g_ref, wu_ref, cw_ref, wd_ref, o_ref, st_ref, ext_ref, a_ref = refs
        hist_ref = None
    if has_mix:
        y = jnp.dot(ya_ref[...], wa_ref[...], preferred_element_type=f32)
        y = y + jnp.dot(yb_ref[...], wb_ref[...], preferred_element_type=f32)
        xs_ref[...] = x_ref[...] + y
        x_ref = xs_ref
    xn = _rmsnorm(x_ref[...], g_ref[...]).astype(bf16)
    _conv_init(ext_ref, hist_ref, width=CONV_FFN, stride=stride, tile=pl.program_id(1))
    for c0 in range(0, D_FF, FFN_CHUNK):
        if c0 == D_FF // FFN_CHUNK // 2 * FFN_CHUNK:
            midway()
        cols = slice(c0, c0 + FFN_CHUNK)
        gate = jnp.dot(xn, wu_ref[:, c0:c0 + FFN_CHUNK], preferred_element_type=f32)
        val = jnp.dot(xn, wu_ref[:, D_FF + c0:D_FF + c0 + FFN_CHUNK], preferred_element_type=f32)
        gconv = _causal_conv(ext_ref, gate, cw_ref, st_ref, width=CONV_FFN, stride=stride, tm=tm,
                             n_tiles=n_tiles, cols=cols)
        a_ref[:, cols] = (gconv * _sigmoid(gconv) * val).astype(bf16)
    o_ref[...] = x_ref[...] + jnp.dot(a_ref[...], wd_ref[...], preferred_element_type=f32)


def _ffn(x, g, wu, cw, wd, hist, *, layer, n_seq, n_tiles, tm, stride, side=None, mix=None):
    n = x.shape[0]
    hp, h = _conv_rows(CONV_FFN, stride)
    tile = lambda c: pl.BlockSpec((tm, c), lambda s, t, *_: (s * n_tiles + t, 0))
    row = tile(D_MODEL)
    slab = lambda a: pl.BlockSpec((None,) + a.shape[1:], lambda *_: (layer, 0, 0), pipeline_mode=pl.Buffered(1))
    in_specs = [row, _const_spec(g.shape), slab(wu), _const_spec(cw.shape), slab(wd)]
    args = [x, g, wu, cw, wd]
    scratch = [pltpu.VMEM((hp + tm, D_FF), f32), pltpu.VMEM((tm, D_FF), bf16)]
    if hist is not None:
        in_specs.append(_const_spec(hist.shape))
        args.append(hist)
    weights = 2 * (wu.size + wd.size) // wu.shape[0]
    if mix is not None:
        ya, yb, wa, wb = mix
        in_specs += [tile(ya.shape[1]), tile(yb.shape[1]), _const_spec(wa.shape), _const_spec(wb.shape)]
        args += [ya, yb, wa, wb]
        scratch.append(pltpu.VMEM((tm, D_MODEL), f32))
        weights += 2 * (wa.size + wb.size)
    vmem = weights + 24 * tm * D_MODEL + (hp + tm) * D_FF * 4 + tm * D_FF * 2 + (12 << 20)
    return _host_call(
        _ffn_kernel, "conv_ffn", grid=(n_seq, n_tiles), in_specs=in_specs, args=args,
        out_specs=[row, pl.BlockSpec((1, h, D_FF), lambda s, t, *_: (s, 0, 0))],
        out_shape=[jax.ShapeDtypeStruct((n, D_MODEL), f32), jax.ShapeDtypeStruct((n_seq, h, D_FF), f32)],
        scratch=scratch, vmem=vmem,
        static=dict(tm=tm, stride=stride, has_hist=hist is not None, has_mix=mix is not None), side=side)


def _conformer_kernel(*refs, tm, stride, n_tiles, has_hist, grouped, midway):
    refs = list(refs)
    group_ref = refs.pop() if grouped else None
    if has_hist:
        (x_ref, g_ref, w1_ref, b1_ref, cw_ref, cb_ref, lg_ref, lb_ref, w2_ref, hist_ref,
         o_ref, st_ref, ext_ref) = refs
    else:
        (x_ref, g_ref, w1_ref, b1_ref, cw_ref, cb_ref, lg_ref, lb_ref, w2_ref,
         o_ref, st_ref, ext_ref) = refs
        hist_ref = None
    xn = _rmsnorm(x_ref[...], g_ref[...]).astype(bf16)
    _conv_init(ext_ref, hist_ref, width=CONV_C, stride=stride, tile=pl.program_id(1))
    parts = []
    for j, c0 in enumerate(range(0, D_C, CONV_CHUNK)):
        if c0 == D_C // 2:
            midway()
        cols = slice(c0, c0 + CONV_CHUNK)
        gcols = slice(D_C + c0, D_C + c0 + CONV_CHUNK)
        a = jnp.dot(xn, w1_ref[:, cols], preferred_element_type=f32) + b1_ref[:, cols]
        g = jnp.dot(xn, w1_ref[:, gcols], preferred_element_type=f32) + b1_ref[:, gcols]
        u = a * _sigmoid(g)
        for i, l0 in enumerate(range(0, CONV_CHUNK, V7X_LANES)):
            lanes = slice(c0 + l0, c0 + l0 + V7X_LANES)
            grp = group_ref.at[j * (CONV_CHUNK // V7X_LANES) + i] if grouped else None
            y = _causal_conv(ext_ref, u[:, l0:l0 + V7X_LANES], cw_ref, st_ref, width=CONV_C, stride=stride, tm=tm,
                             n_tiles=n_tiles, cols=lanes, group_ref=grp)
            parts.append(y + cb_ref[:, lanes])
    y = jnp.concatenate(parts, axis=1)
    mu = jnp.mean(y, axis=-1, keepdims=True)
    yc = y - mu
    yn = yc * lax.rsqrt(jnp.mean(yc * yc, axis=-1, keepdims=True) + EPS) * lg_ref[...] + lb_ref[...]
    a = yn * _sigmoid(yn)
    o_ref[...] = x_ref[...] + jnp.dot(a.astype(bf16), w2_ref[...], preferred_element_type=f32)


def _conformer(x, g, w1, b1, cw, cb, lg, lb, w2, hist, *, n_seq, n_tiles, tm, stride, side=None):
    n = x.shape[0]
    hp, h = _conv_rows(CONV_C, stride)
    grouped = stride % V7X_SUBLANES != 0
    row = pl.BlockSpec((tm, D_MODEL), lambda s, t, *_: (s * n_tiles + t, 0))
    consts = [g, w1, b1, cw, cb, lg, lb, w2] + ([hist] if hist is not None else [])
    scratch = [pltpu.VMEM((hp + tm, D_C), f32)]
    group_bytes = 0
    if grouped:
        group_shape = (D_C // V7X_LANES, V7X_SUBLANES - 1, tm + V7X_SUBLANES, V7X_LANES)
        scratch.append(pltpu.VMEM(group_shape, f32))
        group_bytes = 4 * group_shape[0] * group_shape[1] * group_shape[2] * group_shape[3]
    vmem = 2 * (w1.size + w2.size) + 16 * tm * D_MODEL + 6 * (hp + tm) * D_C * 4 + group_bytes + (8 << 20)
    return _host_call(
        _conformer_kernel, "conformer", grid=(n_seq, n_tiles),
        in_specs=[row] + [_const_spec(a.shape) for a in consts], args=[x] + consts,
        out_specs=[row, pl.BlockSpec((1, h, D_C), lambda s, t, *_: (s, 0, 0))],
        out_shape=[jax.ShapeDtypeStruct((n, D_MODEL), f32), jax.ShapeDtypeStruct((n_seq, h, D_C), f32)],
        scratch=scratch, vmem=vmem,
        static=dict(tm=tm, stride=stride, has_hist=hist is not None, grouped=grouped), side=side)


def _time_major(a):
    return jnp.swapaxes(a, 0, 1).reshape(-1, a.shape[-1])


def _batch_major(a, n_batch):
    return jnp.swapaxes(a.reshape(-1, n_batch, a.shape[-1]), 0, 1)


def _pad_tokens(a):
    return jnp.pad(a, ((0, 0), (0, V7X_SUBLANES - a.shape[1]), (0, 0)))


def _heads_last(a_t, n_seq, seq):
    return jnp.transpose(a_t.reshape(n_seq, N_HEADS, HEAD_DIM, seq), (0, 3, 1, 2))


def kernel(x_prompt, x_sample, cache_k, cache_v, cache_logf, page_table, state_conv_a, state_conv_c, state_ffn, norm_mix0, w_in0, b_forget, conv_a_w, q_norm, k_norm, w_out0, norm_mix1, w_pw1, b_pw1, conv_c_w, conv_c_b, ln_c_g, ln_c_b, w_pw2, norm_ffn, w_up, conv_ffn_w, w_down):
    n_batch, seq, _ = x_prompt.shape
    dec_batch, dec_seq, _ = x_sample.shape
    n_pool = cache_k.shape[1]
    p_tiles = seq // PROMPT_TM
    prompt = dict(n_seq=n_batch, n_tiles=p_tiles, tm=PROMPT_TM, stride=1)
    sample = dict(n_seq=1, n_tiles=1, tm=dec_batch * dec_seq, stride=dec_batch)
    vec = lambda a: a.reshape(1, -1)

    yp = x_prompt.reshape(n_batch * seq, D_MODEL)
    ys = _time_major(x_sample)

    w_in = jnp.pad(w_in0[0], ((0, 0), (0, D_IN_PAD - w_in0.shape[2]))).astype(bf16)
    b_f = jnp.pad(b_forget[0], (0, V7X_LANES - N_HEADS)).reshape(1, V7X_LANES)
    q_g = vec(jnp.tile(q_norm[0], N_HEADS))
    k_g = vec(jnp.tile(k_norm[0], N_HEADS))
    lane_head = jnp.arange(D_B) // HEAD_DIM
    head_mean = jnp.where(lane_head[:, None] == lane_head[None, :], 1.0 / HEAD_DIM, 0.0).astype(bf16)
    w_oa = w_out0[0, :D_A].astype(bf16)
    w_ob = w_out0[0, D_A:].astype(bf16)
    proj = (vec(norm_mix0[0]), w_in, b_f, conv_a_w[0], q_g, k_g, head_mean)

    ya_s, q_ts, k_s, _, v_ts, _, lf_s, _, sca = _in_proj(ys, *proj, _time_major(state_conv_a[0]), **sample)
    to_batch = lambda a: _batch_major(a, dec_batch)
    q_s = to_batch(q_ts[0].T)
    v_s = to_batch(v_ts[0].T)
    side_args = (_pad_tokens(q_s), _pad_tokens(to_batch(k_s)), _pad_tokens(v_s), _pad_tokens(to_batch(lf_s)),
                 jnp.transpose(cache_k[0], (0, 2, 3, 1)).reshape(n_pool, D_B, PAGE_SIZE),
                 jnp.transpose(cache_v[0], (0, 2, 3, 1)).reshape(n_pool, D_B, PAGE_SIZE),
                 jnp.swapaxes(cache_logf[0], 1, 2))
    per_host = n_batch * p_tiles // (page_table.shape[1] // PAGES_PER_STEP)
    assert 4 * per_host == dec_batch
    side = lambda i: (i * per_host, page_table, dec_seq, side_args)
    yb_parts = []

    ya, q_t, k, k_t, v_t, vb_t, lf, lf_t, pca, part = _in_proj(yp, *proj, None, **prompt, side=side(0))
    yb_parts.append(part)
    kt, qtail = _fox_prep(lf, k, n_seq=n_batch, seq=seq)
    yb = _fox_attn(q_t, qtail, kt, vb_t, n_seq=n_batch, seq=seq)
    mixers = (ya, yb.reshape(n_batch * seq, D_B), w_oa, w_ob)
    p_k = _heads_last(k_t, n_batch, seq)[None]
    p_v = _heads_last(v_t, n_batch, seq)[None]
    p_logf = jnp.swapaxes(lf_t, 1, 2)[None]

    wu = w_up.astype(bf16)
    wd = w_down.astype(bf16)
    conf = (vec(norm_mix1[0]), w_pw1[0].astype(bf16), vec(b_pw1[0]), conv_c_w[0], vec(conv_c_b[0]),
            vec(ln_c_g[0]), vec(ln_c_b[0]), w_pw2[0].astype(bf16))
    p_ffn = []
    for layer in range(2):
        if layer == 1:
            yp, pcc, part = _conformer(yp, *conf, None, **prompt, side=side(2))
            yb_parts.append(part)
        yp, pst, part = _ffn(yp, vec(norm_ffn[layer]), wu, conv_ffn_w[layer], wd, None, layer=layer, **prompt,
                             side=side(1 + 2 * layer), mix=mixers if layer == 0 else None)
        yb_parts.append(part)
        p_ffn.append(pst)

    yb_s = _time_major(jnp.concatenate(yb_parts, axis=0)[:, :dec_seq]).astype(bf16)
    ys = _out_proj(ys, ya_s, yb_s, w_oa, w_ob, tm=dec_batch * dec_seq)
    s_k = to_batch(k_s).reshape(1, dec_batch, dec_seq, N_HEADS, HEAD_DIM)
    s_v = v_s.reshape(1, dec_batch, dec_seq, N_HEADS, HEAD_DIM)
    s_logf = to_batch(lf_s)[:, :, :N_HEADS].reshape(1, dec_batch, dec_seq, N_HEADS)
    s_conv_a = to_batch(sca[0])[None]
    s_ffn = []
    for layer in range(2):
        if layer == 1:
            ys, scc = _conformer(ys, *conf, _time_major(state_conv_c[0]), **sample)
        ys, sst = _ffn(ys, vec(norm_ffn[layer]), wu, conv_ffn_w[layer], wd, _time_major(state_ffn[layer]),
                       layer=layer, **sample)
        s_ffn.append(to_batch(sst[0]))

    return (yp.reshape(n_batch, seq, D_MODEL), to_batch(ys),
            pca[None], s_conv_a,
            p_k, s_k, p_v, s_v, p_logf, s_logf,
            pcc[None], to_batch(scc[0])[None],
            jnp.stack(p_ffn), jnp.stack(s_ffn))
```

```python
import functools

import jax
import jax.numpy as jnp
from jax import lax
from jax.experimental import pallas as pl
from jax.experimental.pallas import tpu as pltpu

D_MODEL = 1024
HEAD_DIM = 64
N_HEADS = 8
D_A = 512
D_B = 512
D_C = 1024
D_FF = 2816
CONV_A = 3
CONV_C = 31
CONV_FFN = 3
PAGE_SIZE = 128
EPS = 1e-6

V7X_LANES = 128
V7X_SUBLANES = 8
V7X_VMEM_BYTES = 64 * 1024 * 1024
NEG_BIG = -1e30
LOG2E = 1.4426950408889634

D_IN_PAD = 3 * D_A + 3 * D_B + V7X_LANES
BIAS_LANE = HEAD_DIM
TAIL_ROWS = V7X_SUBLANES
VT_ROWS = HEAD_DIM + 16

PROMPT_TM = 256
FFN_CHUNK = 256
CONV_CHUNK = 256
PREP_TS = 1024
PAGES_PER_STEP = 16

bf16 = jnp.bfloat16
f32 = jnp.float32


def _round_up(n, m):
    return (n + m - 1) // m * m


def _const_spec(shape):
    zeros = (0,) * len(shape)
    return pl.BlockSpec(shape, lambda *_: zeros, pipeline_mode=pl.Buffered(1))


def _params(semantics, vmem_bytes):
    return pltpu.CompilerParams(dimension_semantics=semantics,
                                vmem_limit_bytes=min(int(vmem_bytes), V7X_VMEM_BYTES - (4 << 20)))


def _rmsnorm(x, g):
    return x * lax.rsqrt(jnp.mean(x * x, axis=-1, keepdims=True) + EPS) * g


def _sigmoid(x):
    return 1.0 / (1.0 + jnp.exp(-x))


def _conv_rows(width, stride):
    h = (width - 1) * stride
    return _round_up(h, V7X_SUBLANES), h


def _conv_init(ext_ref, hist_ref, *, width, stride, tile):
    hp, h = _conv_rows(width, stride)

    @pl.when(tile == 0)
    def _():
        if hist_ref is None:
            ext_ref[0:hp, :] = jnp.zeros((hp, ext_ref.shape[1]), f32)
        else:
            ext_ref[hp - h:hp, :] = hist_ref[...]


def _causal_conv(ext_ref, u, w_ref, state_ref, *, width, stride, tm, n_tiles, cols=slice(None), group_ref=None):
    hp, h = _conv_rows(width, stride)
    ext_ref[hp:hp + tm, cols] = u
    starts = [hp - (width - 1 - k) * stride for k in range(width)]
    y = None
    if group_ref is None:
        for k, start in enumerate(starts):
            term = w_ref[k:k + 1, cols] * ext_ref[start:start + tm, cols]
            y = term if y is None else y + term
    else:
        for off in range(V7X_SUBLANES):
            taps = [k for k, start in enumerate(starts) if start % V7X_SUBLANES == off]
            if not taps:
                continue
            rows = tm if off == 0 else tm + V7X_SUBLANES
            part = None
            for k in taps:
                base = starts[k] - off
                term = w_ref[k:k + 1, cols] * ext_ref[base:base + rows, cols]
                part = term if part is None else part + term
            if off:
                group_ref[off - 1, 0:rows, :] = part
                part = group_ref[off - 1, off:off + tm, :]
            y = part if y is None else y + part
    state_ref[0, :, cols] = ext_ref[hp + tm - h:hp + tm, cols]
    if n_tiles > 1:
        ext_ref[0:hp, cols] = ext_ref[tm:tm + hp, cols]
    return y


def _in_proj_kernel(*refs, tm, stride, n_tiles, has_hist, midway):
    if has_hist:
        (x_ref, g_ref, w_ref, bf_ref, cw_ref, qg_ref, kg_ref, bd_ref, hist_ref,
         ya_ref, qt_ref, k_ref, kt_ref, vt_ref, vbt_ref, lf_ref, lft_ref, st_ref, ext_ref) = refs
    else:
        (x_ref, g_ref, w_ref, bf_ref, cw_ref, qg_ref, kg_ref, bd_ref,
         ya_ref, qt_ref, k_ref, kt_ref, vt_ref, vbt_ref, lf_ref, lft_ref, st_ref, ext_ref) = refs
        hist_ref = None
    tile = pl.program_id(1)
    xn = _rmsnorm(x_ref[...], g_ref[...]).astype(bf16)
    _conv_init(ext_ref, hist_ref, width=CONV_A, stride=stride, tile=tile)
    h = jnp.dot(xn, w_ref[:, 3 * D_A:], preferred_element_type=f32)
    q = h[:, 0:D_B]
    k = h[:, D_B:2 * D_B]
    v = h[:, 2 * D_B:3 * D_B]
    fl = h[:, 3 * D_B:]

    def head_norm(z, gain):
        ms = jnp.dot((z * z).astype(bf16), bd_ref[...], preferred_element_type=f32)
        return z * lax.rsqrt(ms + EPS) * gain

    qn = head_norm(q, qg_ref[...]) * (HEAD_DIM ** -0.5)
    kn = head_norm(k, kg_ref[...])
    midway()
    ha = jnp.dot(xn, w_ref[:, 0:3 * D_A], preferred_element_type=f32)
    u = ha[:, D_A:2 * D_A] * ha[:, 2 * D_A:3 * D_A]
    yc = _causal_conv(ext_ref, u, cw_ref, st_ref, width=CONV_A, stride=stride, tm=tm, n_tiles=n_tiles)
    ya_ref[...] = (ha[:, 0:D_A] * yc).astype(bf16)

    qt_ref[0] = qn.T
    k_ref[...] = kn
    kt_ref[0] = kn.T
    vt = v.T
    vt_ref[0] = vt
    ones = jnp.ones((VT_ROWS - HEAD_DIM, vt.shape[1]), f32)
    vbt_ref[0, 0] = jnp.concatenate(
        [blk for hd in range(N_HEADS) for blk in (vt[hd * HEAD_DIM:(hd + 1) * HEAD_DIM, :], ones)], axis=0).astype(bf16)
    z = fl + bf_ref[...]
    logf = jnp.minimum(z, 0.0) - jnp.log(1.0 + jnp.exp(-jnp.abs(z)))
    lane = lax.broadcasted_iota(jnp.int32, logf.shape, 1)
    logf = jnp.where(lane < N_HEADS, logf, 0.0)
    lf_ref[...] = logf
    lft_ref[0] = logf.T[0:N_HEADS, :]


def _in_proj(x, g, w, bfv, cw, qg, kg, bd, hist, *, n_seq, n_tiles, tm, stride, side=None):
    n = x.shape[0]
    seq = n_tiles * tm
    hp, h = _conv_rows(CONV_A, stride)
    row = lambda c: pl.BlockSpec((tm, c), lambda s, t, *_: (s * n_tiles + t, 0))
    col = lambda r: pl.BlockSpec((1, r, tm), lambda s, t, *_: (s, 0, t))
    in_specs = [row(D_MODEL), _const_spec(g.shape), _const_spec(w.shape), _const_spec(bfv.shape),
                _const_spec(cw.shape), _const_spec(qg.shape), _const_spec(kg.shape), _const_spec(bd.shape)]
    args = [x, g, w, bfv, cw, qg, kg, bd]
    if hist is not None:
        in_specs.append(_const_spec(hist.shape))
        args.append(hist)
    out_shape = [jax.ShapeDtypeStruct((n, D_A), bf16),
                 jax.ShapeDtypeStruct((n_seq, D_B, seq), f32),
                 jax.ShapeDtypeStruct((n, D_B), f32),
                 jax.ShapeDtypeStruct((n_seq, D_B, seq), f32),
                 jax.ShapeDtypeStruct((n_seq, D_B, seq), f32),
                 jax.ShapeDtypeStruct((n_seq, n_tiles, N_HEADS * VT_ROWS, tm), bf16),
                 jax.ShapeDtypeStruct((n, V7X_LANES), f32),
                 jax.ShapeDtypeStruct((n_seq, N_HEADS, seq), f32),
                 jax.ShapeDtypeStruct((n_seq, h, D_A), f32)]
    out_specs = [row(D_A), col(D_B), row(D_B), col(D_B), col(D_B),
                 pl.BlockSpec((1, 1, N_HEADS * VT_ROWS, tm), lambda s, t, *_: (s, t, 0, 0)),
                 row(V7X_LANES), col(N_HEADS),
                 pl.BlockSpec((1, h, D_A), lambda s, t, *_: (s, 0, 0))]
    vmem = 2 * w.size + 10 * tm * D_IN_PAD * 4 + (8 << 20)
    return _host_call(_in_proj_kernel, "in_proj", grid=(n_seq, n_tiles), in_specs=in_specs, args=args,
                      out_specs=out_specs, out_shape=out_shape, scratch=[pltpu.VMEM((hp + tm, D_A), f32)],
                      vmem=vmem, static=dict(tm=tm, stride=stride, has_hist=hist is not None), side=side)


def _split3(x):
    hi = x.astype(bf16).astype(f32)
    r = x - hi
    mid = r.astype(bf16).astype(f32)
    return hi, mid, r - mid


def _fox_prep_kernel(lf_ref, k_ref, kt_ref, qtail_ref, carry_ref, *, ts):
    @pl.when(pl.program_id(1) == 0)
    def _():
        carry_ref[...] = jnp.zeros(carry_ref.shape, f32)

    c = lf_ref[...]
    row = lax.broadcasted_iota(jnp.int32, c.shape, 0)
    shift = 1
    while shift < ts:
        c = c + jnp.where(row >= shift, pltpu.roll(c, shift, axis=0), 0.0)
        shift *= 2
    c = c + carry_ref[0:1, :]
    carry_ref[...] = jnp.broadcast_to(c[ts - 1:ts, :], carry_ref.shape)
    c = c * LOG2E
    c_t = c.T

    lane = lax.broadcasted_iota(jnp.int32, (ts, V7X_LANES), 1)
    ones = jnp.where((lane >= BIAS_LANE) & (lane < BIAS_LANE + 3), 1.0, 0.0)
    one_row = jnp.ones((1, ts), f32)
    zero_row = jnp.zeros((1, ts), f32)
    for hd in range(N_HEADS):
        hi, mid, lo = _split3(jnp.broadcast_to(c[:, hd:hd + 1], (ts, V7X_LANES)))
        neg = jnp.where(lane == BIAS_LANE + 3, -hi, jnp.where(lane == BIAS_LANE + 4, -mid,
                        jnp.where(lane == BIAS_LANE + 5, -lo, ones)))
        g0 = (hd // 2) * V7X_LANES
        ks = k_ref[:, g0:g0 + V7X_LANES]
        if hd % 2:
            ks = pltpu.roll(ks, HEAD_DIM, axis=1)
        kt_ref[0, hd] = jnp.where(lane < HEAD_DIM, ks, neg).astype(bf16)
        hi_t, mid_t, lo_t = _split3(c_t[hd:hd + 1, :])
        qtail_ref[0, hd] = jnp.concatenate([hi_t, mid_t, lo_t, one_row, one_row, one_row, zero_row, zero_row],
                                           axis=0)


def _fox_prep(lf, k, *, n_seq, seq):
    ts = PREP_TS
    n_tiles = seq // ts
    row = lambda c: pl.BlockSpec((ts, c), lambda s, t, *_: (s * n_tiles + t, 0))
    return pl.pallas_call(
        functools.partial(_fox_prep_kernel, ts=ts),
        grid=(n_seq, n_tiles), in_specs=[row(V7X_LANES), row(D_B)],
        out_specs=[pl.BlockSpec((1, N_HEADS, ts, V7X_LANES), lambda s, t: (s, 0, t, 0)),
                   pl.BlockSpec((1, N_HEADS, TAIL_ROWS, ts), lambda s, t: (s, 0, 0, t))],
        out_shape=[jax.ShapeDtypeStruct((n_seq, N_HEADS, seq, V7X_LANES), bf16),
                   jax.ShapeDtypeStruct((n_seq, N_HEADS, TAIL_ROWS, seq), f32)],
        scratch_shapes=[pltpu.VMEM((V7X_SUBLANES, V7X_LANES), f32)],
        compiler_params=_params(("arbitrary", "arbitrary"), 40 << 20), name="fox_prep")(lf, k)


def _fox_attn_kernel(qt_ref, qtail_ref, kt_ref, vbt_ref, o_ref, qa_ref, m_ref, acc_ref, sa_ref, sb_ref, *, tq):
    qi = pl.program_id(1)
    pad = jnp.zeros((V7X_LANES - HEAD_DIM - TAIL_ROWS, tq), f32)
    for hd in range(N_HEADS):
        qa = jnp.concatenate([qt_ref[0, hd * HEAD_DIM:(hd + 1) * HEAD_DIM, :] * LOG2E, qtail_ref[0, hd], pad], axis=0)
        qa_ref[hd] = qa.astype(bf16)
    m_ref[...] = jnp.full(m_ref.shape, NEG_BIG, f32)
    acc_ref[...] = jnp.zeros(acc_ref.shape, f32)

    def scores_into(s_ref, j):
        ks = pl.multiple_of(j * tq, tq)
        for hd in range(N_HEADS):
            s_ref[hd] = jnp.dot(kt_ref[0, hd, pl.ds(ks, tq), :], qa_ref[hd],
                                preferred_element_type=f32)

    def consume(s_ref, j, diagonal, next_ref=None):
        ks = pl.multiple_of((j + 1) * tq, tq)
        for hd in range(N_HEADS):
            if next_ref is not None:
                next_ref[hd] = jnp.dot(kt_ref[0, hd, pl.ds(ks, tq), :], qa_ref[hd], preferred_element_type=f32)
            s = s_ref[hd]
            if diagonal:
                key = lax.broadcasted_iota(jnp.int32, s.shape, 0)
                qry = lax.broadcasted_iota(jnp.int32, s.shape, 1)
                s = jnp.where(key <= qry, s, NEG_BIG)
            m_old = m_ref[hd]
            m_new = jnp.maximum(m_old, jnp.max(s, axis=0, keepdims=True))
            alpha = jnp.exp2(m_old - m_new)
            p = jnp.exp2(s - m_new)
            m_ref[hd] = m_new
            vv = vbt_ref[0, j, hd * VT_ROWS:(hd + 1) * VT_ROWS, :]
            acc_ref[hd] = alpha * acc_ref[hd] + jnp.dot(vv, p.astype(bf16), preferred_element_type=f32)

    scores_into(sa_ref, 0)

    def body(i, carry):
        j = 2 * i
        consume(sa_ref, j, False, next_ref=sb_ref)
        consume(sb_ref, j + 1, False, next_ref=sa_ref)
        return carry

    lax.fori_loop(0, qi // 2, body, 0)

    @pl.when(qi % 2 == 0)
    def _():
        consume(sa_ref, qi, True)

    @pl.when(qi % 2 == 1)
    def _():
        consume(sa_ref, qi - 1, False, next_ref=sb_ref)
        consume(sb_ref, qi, True)

    o = jnp.concatenate([acc_ref[hd, 0:HEAD_DIM, :] / acc_ref[hd, HEAD_DIM:HEAD_DIM + 1, :]
                         for hd in range(N_HEADS)], axis=0)
    o_ref[0] = o.T.astype(o_ref.dtype)


def _fox_attn(qt, qtail, kt, vbt, *, n_seq, seq):
    tq = PROMPT_TM
    return pl.pallas_call(
        functools.partial(_fox_attn_kernel, tq=tq),
        grid=(n_seq, seq // tq),
        in_specs=[pl.BlockSpec((1, D_B, tq), lambda b, i: (b, 0, i)),
                  pl.BlockSpec((1, N_HEADS, TAIL_ROWS, tq), lambda b, i: (b, 0, 0, i)),
                  pl.BlockSpec((1, N_HEADS, seq, V7X_LANES), lambda b, i: (b, 0, 0, 0)),
                  pl.BlockSpec((1, seq // tq, N_HEADS * VT_ROWS, tq), lambda b, i: (b, 0, 0, 0))],
        out_specs=pl.BlockSpec((1, tq, D_B), lambda b, i: (b, i, 0)),
        out_shape=jax.ShapeDtypeStruct((n_seq, seq, D_B), bf16),
        scratch_shapes=[pltpu.VMEM((N_HEADS, V7X_LANES, tq), bf16), pltpu.VMEM((N_HEADS, 1, tq), f32),
                        pltpu.VMEM((N_HEADS, VT_ROWS, tq), f32),
                        pltpu.VMEM((N_HEADS, tq, tq), f32), pltpu.VMEM((N_HEADS, tq, tq), f32)],
        compiler_params=_params(("arbitrary", "arbitrary"), 48 << 20), name="fox_attn")(
            qt, qtail, kt, vbt)


def _page_copies(pt_ref, ck_ref, cv_ref, cl_ref, kbuf, vbuf, lbuf, sems, step, slot, *, n_chunks, n_pages, g, b0):
    b = b0 + step // n_chunks
    c = step % n_chunks
    base = b * n_pages + (n_chunks - 1 - c) * g
    copies = []
    for j in range(g):
        page = pt_ref[base + j]
        lanes = pl.ds(j * PAGE_SIZE, PAGE_SIZE)
        copies.append(pltpu.make_async_copy(ck_ref.at[page], kbuf.at[slot, :, lanes], sems.at[0, slot]))
        copies.append(pltpu.make_async_copy(cv_ref.at[page], vbuf.at[slot, :, lanes], sems.at[1, slot]))
        copies.append(pltpu.make_async_copy(cl_ref.at[page], lbuf.at[slot, :, lanes], sems.at[2, slot]))
    return copies


def _dot_f32(x, m01):
    out = None
    for piece in _split3(x):
        t = jnp.dot(piece.astype(bf16), m01, preferred_element_type=f32)
        out = t if out is None else out + t
    return out


N_SIDE_IN = 7
N_SIDE_SCRATCH = 9


def _side_begin(sr, step, *, n_steps, n_chunks, n_pages, g, n_tok, b0):
    (pt_ref, q_ref, kn_ref, vn_ref, lfn_ref, ck_ref, cv_ref, cl_ref, o_ref,
     kbuf, vbuf, lbuf, sems, m_ref, l_ref, acc_ref, run_ref, qbd_ref) = sr
    c = step % n_chunks
    slot = step % 2
    copies = functools.partial(_page_copies, pt_ref, ck_ref, cv_ref, cl_ref, kbuf, vbuf, lbuf, sems,
                               n_chunks=n_chunks, n_pages=n_pages, g=g, b0=b0)
    rows = n_tok * N_HEADS
    contract_last = (((1,), (1,)), ((), ()))

    @pl.when(step + 1 < n_steps)
    def _():
        for cp in copies(step + 1, 1 - slot):
            cp.start()

    @pl.when(c == 0)
    def _():
        @pl.when(step == 0)
        def _():
            for cp in copies(step, slot):
                cp.start()

        lane = lax.broadcasted_iota(jnp.int32, (N_HEADS, D_B), 1)
        head = lax.broadcasted_iota(jnp.int32, (N_HEADS, D_B), 0)
        own = (lane // HEAD_DIM) == head
        q4 = q_ref[0]
        qbd = jnp.concatenate([jnp.where(own, jnp.broadcast_to(q4[t:t + 1, :], (N_HEADS, D_B)), 0.0)
                               for t in range(n_tok)], axis=0)
        qbd_ref[...] = qbd.astype(bf16)
        s = lax.dot_general(qbd.astype(bf16), kn_ref[0].astype(bf16), contract_last,
                            preferred_element_type=f32)
        lfn = lfn_ref[0]
        acc_rows = [lfn[0:1, :]]
        for t in range(1, V7X_SUBLANES):
            acc_rows.append(acc_rows[-1] + lfn[t:t + 1, :])
        dq = jnp.concatenate(acc_rows, axis=0)
        dq_t = dq.T[0:N_HEADS, 0:V7X_SUBLANES]
        bias = jnp.concatenate([dq_t] * n_tok, axis=0)
        r = lax.broadcasted_iota(jnp.int32, (rows, V7X_SUBLANES), 0) // N_HEADS
        t2 = lax.broadcasted_iota(jnp.int32, (rows, V7X_SUBLANES), 1)
        s = jnp.where(t2 <= r, s - bias, NEG_BIG)
        m = jnp.max(s, axis=-1, keepdims=True)
        p = jnp.exp(s - m)
        m_ref[...] = m
        l_ref[...] = jnp.sum(p, axis=-1, keepdims=True)
        acc_ref[...] = jnp.dot(p.astype(bf16), vn_ref[0].astype(bf16), preferred_element_type=f32)
        run_ref[...] = jnp.zeros(run_ref.shape, f32)

    for cp in copies(step, slot):
        cp.wait()
    return slot


def _side_scores(sr, slot, *, g, n_tok):
    (pt_ref, q_ref, kn_ref, vn_ref, lfn_ref, ck_ref, cv_ref, cl_ref, o_ref,
     kbuf, vbuf, lbuf, sems, m_ref, l_ref, acc_ref, run_ref, qbd_ref) = sr
    s = jnp.dot(qbd_ref[...], kbuf[slot].astype(bf16), preferred_element_type=f32)

    lfc = lbuf[slot]
    lfp = jnp.concatenate([lfc[:, j * PAGE_SIZE:(j + 1) * PAGE_SIZE] for j in range(g)], axis=0)
    jj = lax.broadcasted_iota(jnp.int32, (PAGE_SIZE, PAGE_SIZE), 0)
    ss = lax.broadcasted_iota(jnp.int32, (PAGE_SIZE, PAGE_SIZE), 1)
    later = jnp.where(jj > ss, 1.0, 0.0).astype(bf16)
    within = _dot_f32(lfp, later)
    total = jnp.sum(lfp, axis=-1, keepdims=True)
    run = run_ref[:, 0:1]
    tiles = [None] * g
    for j in reversed(range(g)):
        tiles[j] = within[j * N_HEADS:(j + 1) * N_HEADS, :] + run
        run = run + total[j * N_HEADS:(j + 1) * N_HEADS, :]
    run_ref[...] = jnp.broadcast_to(run, run_ref.shape)
    bias = jnp.concatenate(tiles, axis=1)
    return s + jnp.concatenate([bias] * n_tok, axis=0)


def _side_update(sr, s, slot):
    (pt_ref, q_ref, kn_ref, vn_ref, lfn_ref, ck_ref, cv_ref, cl_ref, o_ref,
     kbuf, vbuf, lbuf, sems, m_ref, l_ref, acc_ref, run_ref, qbd_ref) = sr
    contract_last = (((1,), (1,)), ((), ()))
    m_old = m_ref[...]
    m_new = jnp.maximum(m_old, jnp.max(s, axis=-1, keepdims=True))
    alpha = jnp.exp(m_old - m_new)
    p = jnp.exp(s - m_new)
    m_ref[...] = m_new
    l_ref[...] = alpha * l_ref[...] + jnp.sum(p, axis=-1, keepdims=True)
    pv = lax.dot_general(p.astype(bf16), vbuf[slot].astype(bf16), contract_last, preferred_element_type=f32)
    acc_ref[...] = alpha * acc_ref[...] + pv


def _side_end(sr, step, *, n_chunks, n_tok):
    (pt_ref, q_ref, kn_ref, vn_ref, lfn_ref, ck_ref, cv_ref, cl_ref, o_ref,
     kbuf, vbuf, lbuf, sems, m_ref, l_ref, acc_ref, run_ref, qbd_ref) = sr

    @pl.when(step % n_chunks == n_chunks - 1)
    def _():
        lane = lax.broadcasted_iota(jnp.int32, (N_HEADS, D_B), 1)
        head = lax.broadcasted_iota(jnp.int32, (N_HEADS, D_B), 0)
        own = (lane // HEAD_DIM) == head
        o = acc_ref[...] / l_ref[...]
        outs = [jnp.sum(jnp.where(own, o[t * N_HEADS:(t + 1) * N_HEADS, :], 0.0), axis=0, keepdims=True)
                for t in range(n_tok)]
        outs += [jnp.zeros((V7X_SUBLANES - n_tok, D_B), f32)]
        o_ref[0] = jnp.concatenate(outs, axis=0)


def _host_kernel(*refs, body, n_in, n_out, n_tiles, side, **static):
    if side is None:
        body(*refs, n_tiles=n_tiles, midway=lambda: None, **static)
        return
    pt_ref, rest = refs[0], refs[1:]
    host_in, side_in, rest = rest[:n_in], rest[n_in:n_in + N_SIDE_IN], rest[n_in + N_SIDE_IN:]
    host_out, side_out, rest = rest[:n_out], rest[n_out], rest[n_out + 1:]
    host_scratch, side_scratch = rest[:-N_SIDE_SCRATCH], rest[-N_SIDE_SCRATCH:]
    sr = (pt_ref, *side_in, side_out, *side_scratch)
    n_steps = pl.num_programs(0) * n_tiles
    step = pl.program_id(0) * n_tiles + pl.program_id(1)
    slot = _side_begin(sr, step, n_steps=n_steps, **side)
    s = _side_scores(sr, slot, g=side["g"], n_tok=side["n_tok"])
    body(*host_in, *host_out, *host_scratch, n_tiles=n_tiles, midway=functools.partial(_side_update, sr, s, slot),
         **static)
    _side_end(sr, step, n_chunks=side["n_chunks"], n_tok=side["n_tok"])


def _host_call(body, name, *, grid, in_specs, args, out_specs, out_shape, scratch, vmem, static, side=None):
    n_tiles = grid[1]
    kernel = functools.partial(_host_kernel, body=body, n_in=len(in_specs), n_out=len(out_specs), n_tiles=n_tiles,
                               **static)
    if side is None:
        return pl.pallas_call(functools.partial(kernel, side=None), grid=grid, in_specs=in_specs,
                              out_specs=out_specs, out_shape=out_shape, scratch_shapes=scratch,
                              compiler_params=_params(("arbitrary", "arbitrary"), vmem), name=name)(*args)
    b0, page_table, n_tok, side_args = side
    n_pages = page_table.shape[1]
    g = PAGES_PER_STEP
    n_chunks = n_pages // g
    n_side = grid[0] * n_tiles // n_chunks
    assert n_side * n_chunks == grid[0] * n_tiles
    rows = n_tok * N_HEADS
    seq_of = lambda s, t: (s * n_tiles + t) // n_chunks
    tok = lambda c: pl.BlockSpec((1, V7X_SUBLANES, c), lambda s, t, *_: (b0 + seq_of(s, t), 0, 0))
    anyspec = pl.BlockSpec(memory_space=pl.ANY)
    grid_spec = pltpu.PrefetchScalarGridSpec(
        num_scalar_prefetch=1, grid=grid,
        in_specs=list(in_specs) + [tok(D_B), tok(D_B), tok(D_B), tok(V7X_LANES), anyspec, anyspec, anyspec],
        out_specs=list(out_specs) + [pl.BlockSpec((1, V7X_SUBLANES, D_B), lambda s, t, *_: (seq_of(s, t), 0, 0))],
        scratch_shapes=list(scratch) + [
            pltpu.VMEM((2, D_B, g * PAGE_SIZE), f32), pltpu.VMEM((2, D_B, g * PAGE_SIZE), f32),
            pltpu.VMEM((2, N_HEADS, g * PAGE_SIZE), f32), pltpu.SemaphoreType.DMA((3, 2)),
            pltpu.VMEM((rows, 1), f32), pltpu.VMEM((rows, 1), f32), pltpu.VMEM((rows, D_B), f32),
            pltpu.VMEM((N_HEADS, V7X_LANES), f32), pltpu.VMEM((rows, D_B), bf16)])
    side_vmem = 4 * g * PAGE_SIZE * D_B * 4 + 6 * g * PAGE_SIZE * D_B * 2
    cfg = dict(n_chunks=n_chunks, n_pages=n_pages, g=g, n_tok=n_tok, b0=b0)
    return pl.pallas_call(
        functools.partial(kernel, side=cfg), grid_spec=grid_spec,
        out_shape=list(out_shape) + [jax.ShapeDtypeStruct((n_side, V7X_SUBLANES, D_B), f32)],
        compiler_params=_params(("arbitrary", "arbitrary"), vmem + side_vmem), name=name)(
            page_table.reshape(-1), *args, *side_args)


def _out_proj_kernel(x_ref, ya_ref, yb_ref, wa_ref, wb_ref, o_ref):
    y = jnp.dot(ya_ref[...], wa_ref[...], preferred_element_type=f32)
    y = y + jnp.dot(yb_ref[...], wb_ref[...], preferred_element_type=f32)
    o_ref[...] = x_ref[...] + y


def _out_proj(x, ya, yb, wa, wb, *, tm):
    n = x.shape[0]
    row = lambda c: pl.BlockSpec((tm, c), lambda i: (i, 0))
    return pl.pallas_call(
        _out_proj_kernel, grid=(n // tm,),
        in_specs=[row(D_MODEL), row(D_A), row(D_B), _const_spec(wa.shape), _const_spec(wb.shape)],
        out_specs=row(D_MODEL), out_shape=jax.ShapeDtypeStruct((n, D_MODEL), f32),
        compiler_params=_params(("arbitrary",), 32 << 20), name="out_proj")(x, ya, yb, wa, wb)


def _ffn_kernel(*refs, tm, stride, n_tiles, has_hist, has_mix, midway):
    refs = list(refs)
    if has_mix:
        xs_ref = refs.pop()
        ya_ref, yb_ref, wa_ref, wb_ref = refs[5 + has_hist:9 + has_hist]
        del refs[5 + has_hist:9 + has_hist]
    if has_hist:
        x_ref, g_ref, wu_ref, cw_ref, wd_ref, hist_ref, o_ref, st_ref, ext_ref, a_ref = refs
    else:
        x_ref, g_ref, wu_ref, cw_ref, wd_ref, o_ref, st_ref, ext_ref, a_ref = refs
        hist_ref = None
    if has_mix:
        y = jnp.dot(ya_ref[...], wa_ref[...], preferred_element_type=f32)
        y = y + jnp.dot(yb_ref[...], wb_ref[...], preferred_element_type=f32)
        xs_ref[...] = x_ref[...] + y
        x_ref = xs_ref
    xn = _rmsnorm(x_ref[...], g_ref[...]).astype(bf16)
    _conv_init(ext_ref, hist_ref, width=CONV_FFN, stride=stride, tile=pl.program_id(1))
    for c0 in range(0, D_FF, FFN_CHUNK):
        if c0 == D_FF // FFN_CHUNK // 2 * FFN_CHUNK:
            midway()
        cols = slice(c0, c0 + FFN_CHUNK)
        gate = jnp.dot(xn, wu_ref[:, c0:c0 + FFN_CHUNK], preferred_element_type=f32)
        val = jnp.dot(xn, wu_ref[:, D_FF + c0:D_FF + c0 + FFN_CHUNK], preferred_element_type=f32)
        gconv = _causal_conv(ext_ref, gate, cw_ref, st_ref, width=CONV_FFN, stride=stride, tm=tm,
                             n_tiles=n_tiles, cols=cols)
        a_ref[:, cols] = (gconv * _sigmoid(gconv) * val).astype(bf16)
    o_ref[...] = x_ref[...] + jnp.dot(a_ref[...], wd_ref[...], preferred_element_type=f32)


def _ffn(x, g, wu, cw, wd, hist, *, layer, n_seq, n_tiles, tm, stride, side=None, mix=None):
    n = x.shape[0]
    hp, h = _conv_rows(CONV_FFN, stride)
    tile = lambda c: pl.BlockSpec((tm, c), lambda s, t, *_: (s * n_tiles + t, 0))
    row = tile(D_MODEL)
    slab = lambda a: pl.BlockSpec((None,) + a.shape[1:], lambda *_: (layer, 0, 0), pipeline_mode=pl.Buffered(1))
    in_specs = [row, _const_spec(g.shape), slab(wu), _const_spec(cw.shape), slab(wd)]
    args = [x, g, wu, cw, wd]
    scratch = [pltpu.VMEM((hp + tm, D_FF), f32), pltpu.VMEM((tm, D_FF), bf16)]
    if hist is not None:
        in_specs.append(_const_spec(hist.shape))
        args.append(hist)
    weights = 2 * (wu.size + wd.size) // wu.shape[0]
    if mix is not None:
        ya, yb, wa, wb = mix
        in_specs += [tile(ya.shape[1]), tile(yb.shape[1]), _const_spec(wa.shape), _const_spec(wb.shape)]
        args += [ya, yb, wa, wb]
        scratch.append(pltpu.VMEM((tm, D_MODEL), f32))
        weights += 2 * (wa.size + wb.size)
    vmem = weights + 24 * tm * D_MODEL + (hp + tm) * D_FF * 4 + tm * D_FF * 2 + (12 << 20)
    return _host_call(
        _ffn_kernel, "conv_ffn", grid=(n_seq, n_tiles), in_specs=in_specs, args=args,
        out_specs=[row, pl.BlockSpec((1, h, D_FF), lambda s, t, *_: (s, 0, 0))],
        out_shape=[jax.ShapeDtypeStruct((n, D_MODEL), f32), jax.ShapeDtypeStruct((n_seq, h, D_FF), f32)],
        scratch=scratch, vmem=vmem,
        static=dict(tm=tm, stride=stride, has_hist=hist is not None, has_mix=mix is not None), side=side)


def _conformer_kernel(*refs, tm, stride, n_tiles, has_hist, grouped, midway):
    refs = list(refs)
    group_ref = refs.pop() if grouped else None
    if has_hist:
        (x_ref, g_ref, w1_ref, b1_ref, cw_ref, cb_ref, lg_ref, lb_ref, w2_ref, hist_ref,
         o_ref, st_ref, ext_ref) = refs
    else:
        (x_ref, g_ref, w1_ref, b1_ref, cw_ref, cb_ref, lg_ref, lb_ref, w2_ref,
         o_ref, st_ref, ext_ref) = refs
        hist_ref = None
    xn = _rmsnorm(x_ref[...], g_ref[...]).astype(bf16)
    _conv_init(ext_ref, hist_ref, width=CONV_C, stride=stride, tile=pl.program_id(1))
    parts = []
    for j, c0 in enumerate(range(0, D_C, CONV_CHUNK)):
        if c0 == D_C // 2:
            midway()
        cols = slice(c0, c0 + CONV_CHUNK)
        gcols = slice(D_C + c0, D_C + c0 + CONV_CHUNK)
        a = jnp.dot(xn, w1_ref[:, cols], preferred_element_type=f32) + b1_ref[:, cols]
        g = jnp.dot(xn, w1_ref[:, gcols], preferred_element_type=f32) + b1_ref[:, gcols]
        u = a * _sigmoid(g)
        for i, l0 in enumerate(range(0, CONV_CHUNK, V7X_LANES)):
            lanes = slice(c0 + l0, c0 + l0 + V7X_LANES)
            grp = group_ref.at[j * (CONV_CHUNK // V7X_LANES) + i] if grouped else None
            y = _causal_conv(ext_ref, u[:, l0:l0 + V7X_LANES], cw_ref, st_ref, width=CONV_C, stride=stride, tm=tm,
                             n_tiles=n_tiles, cols=lanes, group_ref=grp)
            parts.append(y + cb_ref[:, lanes])
    y = jnp.concatenate(parts, axis=1)
    mu = jnp.mean(y, axis=-1, keepdims=True)
    yc = y - mu
    yn = yc * lax.rsqrt(jnp.mean(yc * yc, axis=-1, keepdims=True) + EPS) * lg_ref[...] + lb_ref[...]
    a = yn * _sigmoid(yn)
    o_ref[...] = x_ref[...] + jnp.dot(a.astype(bf16), w2_ref[...], preferred_element_type=f32)


def _conformer(x, g, w1, b1, cw, cb, lg, lb, w2, hist, *, n_seq, n_tiles, tm, stride, side=None):
    n = x.shape[0]
    hp, h = _conv_rows(CONV_C, stride)
    grouped = stride % V7X_SUBLANES != 0
    row = pl.BlockSpec((tm, D_MODEL), lambda s, t, *_: (s * n_tiles + t, 0))
    consts = [g, w1, b1, cw, cb, lg, lb, w2] + ([hist] if hist is not None else [])
    scratch = [pltpu.VMEM((hp + tm, D_C), f32)]
    group_bytes = 0
    if grouped:
        group_shape = (D_C // V7X_LANES, V7X_SUBLANES - 1, tm + V7X_SUBLANES, V7X_LANES)
        scratch.append(pltpu.VMEM(group_shape, f32))
        group_bytes = 4 * group_shape[0] * group_shape[1] * group_shape[2] * group_shape[3]
    vmem = 2 * (w1.size + w2.size) + 16 * tm * D_MODEL + 6 * (hp + tm) * D_C * 4 + group_bytes + (8 << 20)
    return _host_call(
        _conformer_kernel, "conformer", grid=(n_seq, n_tiles),
        in_specs=[row] + [_const_spec(a.shape) for a in consts], args=[x] + consts,
        out_specs=[row, pl.BlockSpec((1, h, D_C), lambda s, t, *_: (s, 0, 0))],
        out_shape=[jax.ShapeDtypeStruct((n, D_MODEL), f32), jax.ShapeDtypeStruct((n_seq, h, D_C), f32)],
        scratch=scratch, vmem=vmem,
        static=dict(tm=tm, stride=stride, has_hist=hist is not None, grouped=grouped), side=side)


def _time_major(a):
    return jnp.swapaxes(a, 0, 1).reshape(-1, a.shape[-1])


def _batch_major(a, n_batch):
    return jnp.swapaxes(a.reshape(-1, n_batch, a.shape[-1]), 0, 1)


def _pad_tokens(a):
    return jnp.pad(a, ((0, 0), (0, V7X_SUBLANES - a.shape[1]), (0, 0)))


def _heads_last(a_t, n_seq, seq):
    return jnp.transpose(a_t.reshape(n_seq, N_HEADS, HEAD_DIM, seq), (0, 3, 1, 2))


def kernel(x_prompt, x_sample, cache_k, cache_v, cache_logf, page_table, state_conv_a, state_conv_c, state_ffn, norm_mix0, w_in0, b_forget, conv_a_w, q_norm, k_norm, w_out0, norm_mix1, w_pw1, b_pw1, conv_c_w, conv_c_b, ln_c_g, ln_c_b, w_pw2, norm_ffn, w_up, conv_ffn_w, w_down):
    n_batch, seq, _ = x_prompt.shape
    dec_batch, dec_seq, _ = x_sample.shape
    n_pool = cache_k.shape[1]
    p_tiles = seq // PROMPT_TM
    prompt = dict(n_seq=n_batch, n_tiles=p_tiles, tm=PROMPT_TM, stride=1)
    sample = dict(n_seq=1, n_tiles=1, tm=dec_batch * dec_seq, stride=dec_batch)
    vec = lambda a: a.reshape(1, -1)

    yp = x_prompt.reshape(n_batch * seq, D_MODEL)
    ys = _time_major(x_sample)

    w_in = jnp.pad(w_in0[0], ((0, 0), (0, D_IN_PAD - w_in0.shape[2]))).astype(bf16)
    b_f = jnp.pad(b_forget[0], (0, V7X_LANES - N_HEADS)).reshape(1, V7X_LANES)
    q_g = vec(jnp.tile(q_norm[0], N_HEADS))
    k_g = vec(jnp.tile(k_norm[0], N_HEADS))
    lane_head = jnp.arange(D_B) // HEAD_DIM
    head_mean = jnp.where(lane_head[:, None] == lane_head[None, :], 1.0 / HEAD_DIM, 0.0).astype(bf16)
    w_oa = w_out0[0, :D_A].astype(bf16)
    w_ob = w_out0[0, D_A:].astype(bf16)
    proj = (vec(norm_mix0[0]), w_in, b_f, conv_a_w[0], q_g, k_g, head_mean)

    ya_s, q_ts, k_s, _, v_ts, _, lf_s, _, sca = _in_proj(ys, *proj, _time_major(state_conv_a[0]), **sample)
    to_batch = lambda a: _batch_major(a, dec_batch)
    q_s = to_batch(q_ts[0].T)
    v_s = to_batch(v_ts[0].T)
    side_args = (_pad_tokens(q_s), _pad_tokens(to_batch(k_s)), _pad_tokens(v_s), _pad_tokens(to_batch(lf_s)),
                 jnp.transpose(cache_k[0], (0, 2, 3, 1)).reshape(n_pool, D_B, PAGE_SIZE),
                 jnp.transpose(cache_v[0], (0, 2, 3, 1)).reshape(n_pool, D_B, PAGE_SIZE),
                 jnp.swapaxes(cache_logf[0], 1, 2))
    per_host = n_batch * p_tiles // (page_table.shape[1] // PAGES_PER_STEP)
    assert 4 * per_host == dec_batch
    side = lambda i: (i * per_host, page_table, dec_seq, side_args)
    yb_parts = []

    ya, q_t, k, k_t, v_t, vb_t, lf, lf_t, pca, part = _in_proj(yp, *proj, None, **prompt, side=side(0))
    yb_parts.append(part)
    kt, qtail = _fox_prep(lf, k, n_seq=n_batch, seq=seq)
    yb = _fox_attn(q_t, qtail, kt, vb_t, n_seq=n_batch, seq=seq)
    mixers = (ya, yb.reshape(n_batch * seq, D_B), w_oa, w_ob)
    p_k = _heads_last(k_t, n_batch, seq)[None]
    p_v = _heads_last(v_t, n_batch, seq)[None]
    p_logf = jnp.swapaxes(lf_t, 1, 2)[None]

    wu = w_up.astype(bf16)
    wd = w_down.astype(bf16)
    conf = (vec(norm_mix1[0]), w_pw1[0].astype(bf16), vec(b_pw1[0]), conv_c_w[0], vec(conv_c_b[0]),
            vec(ln_c_g[0]), vec(ln_c_b[0]), w_pw2[0].astype(bf16))
    p_ffn = []
    for layer in range(2):
        if layer == 1:
            yp, pcc, part = _conformer(yp, *conf, None, **prompt, side=side(2))
            yb_parts.append(part)
        yp, pst, part = _ffn(yp, vec(norm_ffn[layer]), wu, conv_ffn_w[layer], wd, None, layer=layer, **prompt,
                             side=side(1 + 2 * layer), mix=mixers if layer == 0 else None)
        yb_parts.append(part)
        p_ffn.append(pst)

    yb_s = _time_major(jnp.concatenate(yb_parts, axis=0)[:, :dec_seq]).astype(bf16)
    ys = _out_proj(ys, ya_s, yb_s, w_oa, w_ob, tm=dec_batch * dec_seq)
    s_k = to_batch(k_s).reshape(1, dec_batch, dec_seq, N_HEADS, HEAD_DIM)
    s_v = v_s.reshape(1, dec_batch, dec_seq, N_HEADS, HEAD_DIM)
    s_logf = to_batch(lf_s)[:, :, :N_HEADS].reshape(1, dec_batch, dec_seq, N_HEADS)
    s_conv_a = to_batch(sca[0])[None]
    s_ffn = []
    for layer in range(2):
        if layer == 1:
            ys, scc = _conformer(ys, *conf, _time_major(state_conv_c[0]), **sample)
        ys, sst = _ffn(ys, vec(norm_ffn[layer]), wu, conv_ffn_w[layer], wd, _time_major(state_ffn[layer]),
                       layer=layer, **sample)
        s_ffn.append(to_batch(sst[0]))

    return (yp.reshape(n_batch, seq, D_MODEL), to_batch(ys),
            pca[None], s_conv_a,
            p_k, s_k, p_v, s_v, p_logf, s_logf,
            pcc[None], to_batch(scc[0])[None],
            jnp.stack(p_ffn), jnp.stack(s_ffn))
```
